```python
import math
import jax, jax.numpy as jnp
from jax import lax
import numpy as np

D_MODEL = 1024
BATCH = 16
SEQ = 2048
DEPTH = 4

GRID_W = 64
CTX_LEN = 256
MIXERS = ("s5", "conv")
N_MIX = len(MIXERS)
S5_GROUP = 16
S5_GROUPS = D_MODEL // S5_GROUP
S5_STATE = 64
N_DIR = 2
CONV_WIDTH = 31
CONV_HALF = CONV_WIDTH // 2
D_FF = 4 * D_MODEL
N_S5_LAYERS = sum(1 for _i in range(DEPTH) if MIXERS[_i % N_MIX] == "s5")
N_CONV_LAYERS = DEPTH - N_S5_LAYERS
DN_ALPHA = (2.0 * DEPTH) ** 0.25
DN_BETA = (8.0 * DEPTH) ** -0.25
LN_EPS = 1e-5
DT_MIN = 1e-3
DT_MAX = 1e-1
POS_TEMP = 10000.0
LAMBDA_RE_MAX = -1e-4

kernel_name = "hybrid_s5_conformer_dit_trunk"


def layer_norm(x, g, b):
    xf = x.astype(jnp.float32)
    mu = jnp.mean(xf, axis=-1, keepdims=True)
    var = jnp.mean(jnp.square(xf - mu), axis=-1, keepdims=True)
    y = (xf - mu) * lax.rsqrt(var + LN_EPS) * g.astype(jnp.float32) + b.astype(jnp.float32)
    return y.astype(x.dtype)


def modulate(x, shift, scale):
    return x * (1 + scale) + shift


def sincos_1d(pos, dim):
    quarter = dim // 2
    omega = POS_TEMP ** (-jnp.arange(quarter, dtype=jnp.float32) / quarter)
    ang = pos[:, None] * omega[None, :]
    return jnp.concatenate([jnp.sin(ang), jnp.cos(ang)], axis=-1)


def grid_pos_embed(rows, dim):
    row_idx = jnp.repeat(jnp.arange(rows), GRID_W).astype(jnp.float32)
    col_idx = jnp.tile(jnp.arange(GRID_W), rows).astype(jnp.float32)
    return jnp.concatenate([sincos_1d(row_idx, dim // 2), sincos_1d(col_idx, dim // 2)], axis=-1)


def s5_discretise(lam_re, lam_im, log_dt, b_re, b_im):
    lr = jnp.minimum(lam_re.astype(jnp.float32), LAMBDA_RE_MAX)
    li = lam_im.astype(jnp.float32)
    dt = jnp.exp(log_dt.astype(jnp.float32))[:, None]
    mag = jnp.exp(lr * dt)
    ab_re = mag * jnp.cos(li * dt)
    ab_im = mag * jnp.sin(li * dt)
    den = lr * lr + li * li
    nr = ab_re - 1.0
    ni = ab_im
    coef_re = (nr * lr + ni * li) / den
    coef_im = (ni * lr - nr * li) / den
    br = b_re.astype(jnp.float32)
    bi = b_im.astype(jnp.float32)
    bb_re = coef_re[..., None] * br - coef_im[..., None] * bi
    bb_im = coef_re[..., None] * bi + coef_im[..., None] * br
    return ab_re, ab_im, bb_re, bb_im


def _scan_op(e1, e2):
    ar1, ai1, br1, bi1 = e1
    ar2, ai2, br2, bi2 = e2
    return (ar2 * ar1 - ai2 * ai1,
            ar2 * ai1 + ai2 * ar1,
            ar2 * br1 - ai2 * bi1 + br2,
            ar2 * bi1 + ai2 * br1 + bi2)


def s5_scan(u, lam_re, lam_im, log_dt, b_re, b_im, h0s):
    length = u.shape[1]
    states = []
    for d in range(N_DIR):
        reverse = d == 1
        ab_re, ab_im, bb_re, bb_im = s5_discretise(lam_re[d], lam_im[d], log_dt[d], b_re[d], b_im[d])
        bu_re = jnp.einsum("blgc,gpc->blgp", u, bb_re)
        bu_im = jnp.einsum("blgc,gpc->blgp", u, bb_im)
        if h0s is not None:
            h0_re, h0_im = h0s[d]
            edge = -1 if reverse else 0
            bu_re = bu_re.at[:, edge].add(ab_re * h0_re - ab_im * h0_im)
            bu_im = bu_im.at[:, edge].add(ab_re * h0_im + ab_im * h0_re)
        a_re = jnp.broadcast_to(ab_re[None, None], (1, length) + ab_re.shape)
        a_im = jnp.broadcast_to(ab_im[None, None], (1, length) + ab_im.shape)
        _, _, h_re, h_im = lax.associative_scan(_scan_op, (a_re, a_im, bu_re, bu_im), reverse=reverse, axis=1)
        states.append((h_re, h_im))
    return states


def s5_readout(u, states, c_re, c_im, d_skip, w_glu, b_glu, out_dtype):
    y = d_skip.astype(jnp.float32).reshape(S5_GROUPS, S5_GROUP) * u
    for d, (h_re, h_im) in enumerate(states):
        y = y + jnp.einsum("blgp,gcp->blgc", h_re, c_re[d].astype(jnp.float32)) \
              - jnp.einsum("blgp,gcp->blgc", h_im, c_im[d].astype(jnp.float32))
    bsz, length = u.shape[0], u.shape[1]
    z = jax.nn.gelu(y.reshape(bsz, length, D_MODEL), approximate=False).astype(out_dtype)
    zz = z @ w_glu + b_glu
    return zz[..., :D_MODEL] * jax.nn.sigmoid(zz[..., D_MODEL:])


def to_groups(h):
    return h.astype(jnp.float32).reshape(h.shape[0], h.shape[1], S5_GROUPS, S5_GROUP)


def conv_module(h, w_pw1, b_pw1, w_dw, b_dw, ln_g, ln_b, w_pw2, b_pw2):
    a = h @ w_pw1 + b_pw1
    a = a[..., :D_MODEL] * jax.nn.sigmoid(a[..., D_MODEL:])
    a = lax.conv_general_dilated(a, w_dw[:, None, :].astype(a.dtype), window_strides=(1,),
                                 padding=[(CONV_HALF, CONV_HALF)],
                                 dimension_numbers=("NWC", "WIO", "NWC"),
                                 feature_group_count=D_MODEL) + b_dw
    a = jax.nn.silu(layer_norm(a, ln_g, ln_b))
    return a @ w_pw2 + b_pw2


def sq_relu_mlp(h, w1, w2):
    return jnp.square(jax.nn.relu(h @ w1)) @ w2


def setup_inputs(seed: int = 0) -> dict:
    key = jax.random.key(seed)
    ks = jax.random.split(key, 32)
    f32 = jnp.float32
    D, G, P, CH = D_MODEL, S5_GROUPS, S5_STATE, S5_GROUP
    nrm = lambda k, shape, s: jax.random.normal(k, shape, f32) * s
    x = nrm(ks[0], (BATCH, SEQ, D), 1.0)
    c = nrm(ks[1], (BATCH, D), 1.0)
    ctx = nrm(ks[2], (BATCH, CTX_LEN, D), 1.0)
    c_ctx = nrm(ks[3], (D,), 1.0)
    w_ada = nrm(ks[4], (DEPTH, D, 6 * D), D ** -0.5)
    b_ada = nrm(ks[5], (DEPTH, 6 * D), 0.02)
    ln_gain = 1.0 + nrm(ks[6], (DEPTH, 2, D), 0.02)
    ln_bias = nrm(ks[7], (DEPTH, 2, D), 0.02)
    n_idx = jnp.arange(P, dtype=f32)
    s5_lam_re = -0.5 + nrm(ks[8], (N_S5_LAYERS, N_DIR, G, P), 0.01)
    s5_lam_im = math.pi * n_idx + nrm(ks[9], (N_S5_LAYERS, N_DIR, G, P), 0.01)
    s5_log_dt = jax.random.uniform(ks[10], (N_S5_LAYERS, N_DIR, G), f32, math.log(DT_MIN), math.log(DT_MAX))
    s5_b_re = nrm(ks[11], (N_S5_LAYERS, N_DIR, G, P, CH), (2.0 * CH) ** -0.5)
    s5_b_im = nrm(ks[12], (N_S5_LAYERS, N_DIR, G, P, CH), (2.0 * CH) ** -0.5)
    s5_c_re = nrm(ks[13], (N_S5_LAYERS, N_DIR, G, CH, P), P ** -0.5)
    s5_c_im = nrm(ks[14], (N_S5_LAYERS, N_DIR, G, CH, P), P ** -0.5)
    s5_d = 1.0 + nrm(ks[15], (N_S5_LAYERS, D), 0.1)
    glu_out = nrm(ks[16], (N_S5_LAYERS, D, D), DN_BETA * D ** -0.5)
    glu_gate = nrm(ks[17], (N_S5_LAYERS, D, D), D ** -0.5)
    s5_w_glu = jnp.concatenate([glu_out, glu_gate], axis=-1)
    s5_b_glu = nrm(ks[18], (N_S5_LAYERS, 2 * D), 0.02)
    cv_w_pw1 = nrm(ks[19], (N_CONV_LAYERS, D, 2 * D), D ** -0.5)
    cv_b_pw1 = nrm(ks[20], (N_CONV_LAYERS, 2 * D), 0.02)
    cv_w_dw = nrm(ks[21], (N_CONV_LAYERS, CONV_WIDTH, D), CONV_WIDTH ** -0.5)
    cv_b_dw = nrm(ks[22], (N_CONV_LAYERS, D), 0.02)
    cv_ln_g = 1.0 + nrm(ks[23], (N_CONV_LAYERS, D), 0.02)
    cv_ln_b = nrm(ks[24], (N_CONV_LAYERS, D), 0.02)
    cv_w_pw2 = nrm(ks[25], (N_CONV_LAYERS, D, D), DN_BETA * D ** -0.5)
    cv_b_pw2 = nrm(ks[26], (N_CONV_LAYERS, D), 0.02)
    mlp_w1 = nrm(ks[27], (DEPTH, D, D_FF), D ** -0.5)
    mlp_w2 = nrm(ks[28], (DEPTH, D_FF, D), DN_BETA * D_FF ** -0.5)
    return {"x": x, "c": c, "ctx": ctx, "c_ctx": c_ctx,
            "w_ada": w_ada, "b_ada": b_ada, "ln_gain": ln_gain, "ln_bias": ln_bias,
            "s5_lam_re": s5_lam_re, "s5_lam_im": s5_lam_im, "s5_log_dt": s5_log_dt,
            "s5_b_re": s5_b_re, "s5_b_im": s5_b_im, "s5_c_re": s5_c_re, "s5_c_im": s5_c_im,
            "s5_d": s5_d, "s5_w_glu": s5_w_glu, "s5_b_glu": s5_b_glu,
            "cv_w_pw1": cv_w_pw1, "cv_b_pw1": cv_b_pw1, "cv_w_dw": cv_w_dw, "cv_b_dw": cv_b_dw,
            "cv_ln_g": cv_ln_g, "cv_ln_b": cv_ln_b, "cv_w_pw2": cv_w_pw2, "cv_b_pw2": cv_b_pw2,
            "mlp_w1": mlp_w1, "mlp_w2": mlp_w2}


def reference(x, c, ctx, c_ctx, w_ada, b_ada, ln_gain, ln_bias,
              s5_lam_re, s5_lam_im, s5_log_dt, s5_b_re, s5_b_im, s5_c_re, s5_c_im,
              s5_d, s5_w_glu, s5_b_glu,
              cv_w_pw1, cv_b_pw1, cv_w_dw, cv_b_dw, cv_ln_g, cv_ln_b, cv_w_pw2, cv_b_pw2,
              mlp_w1, mlp_w2):
    rows = x.shape[1] // GRID_W
    x = x + grid_pos_embed(rows, D_MODEL).astype(x.dtype)[None]
    cond = jax.nn.silu(c)
    cond_ctx = jax.nn.silu(c_ctx)
    kinds = [MIXERS[i % N_MIX] for i in range(DEPTH)]
    s5_j = 0
    cv_j = 0
    for i in range(DEPTH):
        kind = kinds[i]
        ctx_read_here = kind == "s5"
        ctx_needed_later = any(k == "s5" for k in kinds[i + 1:])
        use_ctx = ctx_read_here or ctx_needed_later
        mod = (cond @ w_ada[i] + b_ada[i])[:, None, :]
        sh1, sc1, g1, sh2, sc2, g2 = jnp.split(mod, 6, axis=-1)
        if use_ctx:
            mod_c = (cond_ctx @ w_ada[i] + b_ada[i])[None, None, :]
            csh1, csc1, cg1, csh2, csc2, cg2 = jnp.split(mod_c, 6, axis=-1)
            hc = modulate(ctx, csh1, csc1)
        h = modulate(x, sh1, sc1)
        if kind == "s5":
            j = s5_j
            s5_j += 1
            uc = to_groups(hc)
            states_c = s5_scan(uc, s5_lam_re[j], s5_lam_im[j], s5_log_dt[j], s5_b_re[j], s5_b_im[j], None)
            h0s = [(states_c[0][0][:, -1], states_c[0][1][:, -1]),
                   (states_c[1][0][:, 0], states_c[1][1][:, 0])]
            u = to_groups(h)
            states = s5_scan(u, s5_lam_re[j], s5_lam_im[j], s5_log_dt[j], s5_b_re[j], s5_b_im[j], h0s)
            mix = s5_readout(u, states, s5_c_re[j], s5_c_im[j], s5_d[j], s5_w_glu[j], s5_b_glu[j], x.dtype)
            if ctx_needed_later:
                mix_c = s5_readout(uc, states_c, s5_c_re[j], s5_c_im[j], s5_d[j], s5_w_glu[j], s5_b_glu[j], ctx.dtype)
        else:
            j = cv_j
            cv_j += 1
            cv_args = (cv_w_pw1[j], cv_b_pw1[j], cv_w_dw[j], cv_b_dw[j], cv_ln_g[j], cv_ln_b[j], cv_w_pw2[j], cv_b_pw2[j])
            mix = conv_module(h, *cv_args)
            if ctx_needed_later:
                mix_c = conv_module(hc, *cv_args)
        x = layer_norm(DN_ALPHA * x + g1 * mix, ln_gain[i, 0], ln_bias[i, 0])
        h = modulate(x, sh2, sc2)
        x = layer_norm(DN_ALPHA * x + g2 * sq_relu_mlp(h, mlp_w1[i], mlp_w2[i]), ln_gain[i, 1], ln_bias[i, 1])
        if ctx_needed_later:
            ctx = layer_norm(DN_ALPHA * ctx + cg1 * mix_c, ln_gain[i, 0], ln_bias[i, 0])
            hc2 = modulate(ctx, csh2, csc2)
            ctx = layer_norm(DN_ALPHA * ctx + cg2 * sq_relu_mlp(hc2, mlp_w1[i], mlp_w2[i]), ln_gain[i, 1], ln_bias[i, 1])
    return x
```

```python
import functools
import math

import jax
import jax.numpy as jnp
from jax import lax
from jax.experimental import pallas as pl
from jax.experimental.pallas import tpu as pltpu

F32 = jnp.float32
BF16 = jnp.bfloat16

LANES = 128
S5_CH = 16
CHUNK = 16
CK = CHUNK * S5_CH
SLAB_G = LANES // S5_CH
NQ = 8
NJ = NQ * CHUNK
ROWS = NJ * CHUNK
TAB_ROWS = 40
GRID_W = 64
POS_TEMP = 10000.0
LN_EPS = 1e-5
LAMBDA_RE_MAX = -1e-4
VMEM_LIMIT = 56 * 1024 * 1024
HI = lax.Precision.HIGHEST


def _ln(v, g, b):
    mu = jnp.mean(v, axis=-1, keepdims=True)
    d = v - mu
    var = jnp.mean(d * d, axis=-1, keepdims=True)
    return d * lax.rsqrt(var + LN_EPS) * g + b


def _ada_kernel(cc_ref, w_ref, b_ref, o_ref):
    cc = cc_ref[...]
    s = cc * jax.nn.sigmoid(cc)
    o_ref[...] = jnp.dot(s, w_ref[...], preferred_element_type=F32, precision=HI) + b_ref[...]


def _ada_mods(cc, w_ada, b_ada):
    depth, d, d6 = w_ada.shape
    rows = cc.shape[0]
    tn = 1536 if d6 % 1536 == 0 else d6
    return pl.pallas_call(
        _ada_kernel,
        out_shape=jax.ShapeDtypeStruct((depth, rows, d6), F32),
        grid=(depth, d6 // tn),
        in_specs=[
            pl.BlockSpec((rows, d), lambda i, j: (0, 0)),
            pl.BlockSpec((None, d, tn), lambda i, j: (i, 0, j)),
            pl.BlockSpec((None, 1, tn), lambda i, j: (i, 0, j)),
        ],
        out_specs=pl.BlockSpec((None, rows, tn), lambda i, j: (i, 0, j)),
        compiler_params=pltpu.CompilerParams(vmem_limit_bytes=VMEM_LIMIT),
        name="ada_mods",
    )(cc, w_ada, b_ada.reshape(depth, 1, d6))


def _s5_core_kernel(*refs, chain, has_pos):
    it = iter(refs)
    x_ref = next(it)
    pos_ref = next(it) if has_pos else None
    mod_ref = next(it)
    dsk_ref = next(it)
    mi_ref = next(it)
    ws_ref = next(it)
    wo_ref = next(it)
    tab_ref = next(it)
    h0_ref = next(it) if chain else None
    z_ref = next(it)
    hfin_ref = None if chain else next(it)
    u_ref, r_ref, s_ref, e_ref, yt_ref, y_ref, f_ref, e16_ref = it

    x = x_ref[...]
    if has_pos:
        x = x + pos_ref[...]
    mod = mod_ref[...]
    u_ref[...] = x * (1.0 + mod[1:2]) + mod[0:1]

    for s in range(CHUNK):
        xs = u_ref[pl.ds(s, NJ, stride=CHUNK), :]
        r_ref[:, s] = xs.T.reshape(SLAB_G, S5_CH, NJ).astype(BF16)

    def mm_body(g, carry):
        rg = r_ref[g].reshape(CK, NJ)
        yt_ref[g] = jnp.dot(mi_ref[g], rg, preferred_element_type=F32)
        st = lax.dot_general(rg, ws_ref[g], (((0,), (0,)), ((), ())),
                             preferred_element_type=F32)
        s_ref[g, 0] = st[:, :LANES]
        s_ref[g, 1] = st[:, LANES:]
        return carry

    lax.fori_loop(0, SLAB_G, mm_body, 0)

    def cmul(a1, a2, v):
        return a1 * v + a2 * pltpu.roll(v, LANES // 2, 1)

    def scan_dir(g, d):
        tab = tab_ref.at[g, d]
        sg = s_ref.at[g, d]
        eg = e_ref.at[g, d]
        gd = g * 2 + d
        a1 = tab[0:1, :]
        a2 = tab[1:2, :]
        e = jnp.zeros((NQ, LANES), F32)
        order = range(CHUNK) if d == 0 else range(CHUNK - 1, -1, -1)
        eloc = []
        for jp in order:
            eloc.append((jp, e))
            e = cmul(a1, a2, e) + sg[pl.ds(jp, NQ, stride=CHUNK), :]
        if not chain:
            for jp, el in eloc:
                eg[pl.ds(jp, NQ, stride=CHUNK), :] = el
            hfin_ref[gd] = e
            return
        f_ref[...] = e
        b1 = tab[2:3, :]
        b2 = tab[3:4, :]
        cst = jnp.broadcast_to(h0_ref[pl.ds(gd, 1), :], (NQ, LANES))
        qs = range(NQ) if d == 0 else range(NQ - 1, -1, -1)
        for q in qs:
            e16_ref[q:q + 1, :] = cst[0:1, :]
            cst = cmul(b1, b2, cst) + f_ref[q:q + 1, :]
        e16 = e16_ref[...]
        e16r = pltpu.roll(e16, LANES // 2, 1)
        for jp, el in eloc:
            p1 = tab[8 + jp:9 + jp, :]
            p2 = tab[24 + jp:25 + jp, :]
            eg[pl.ds(jp, NQ, stride=CHUNK), :] = el + p1 * e16 + p2 * e16r

    def scan_body(g, carry):
        scan_dir(g, 0)
        scan_dir(g, 1)
        return carry

    lax.fori_loop(0, SLAB_G, scan_body, 0)

    def out_body(g, carry):
        et = jnp.concatenate([e_ref[g, 0], e_ref[g, 1]], axis=1).astype(BF16)
        yo = lax.dot_general(wo_ref[g], et, (((1,), (1,)), ((), ())),
                             preferred_element_type=F32)
        yt_ref[g] = yt_ref[g] + yo
        return carry

    lax.fori_loop(0, SLAB_G, out_body, 0)

    for t in range(CHUNK):
        zt = yt_ref[:, t * S5_CH:(t + 1) * S5_CH, :].reshape(LANES, NJ)
        y_ref[pl.ds(t, NJ, stride=CHUNK), :] = zt.T

    y = y_ref[...] + dsk_ref[...] * u_ref[...]
    z_ref[...] = (0.5 * y * (1.0 + lax.erf(y * (1.0 / math.sqrt(2.0))))).astype(BF16)


def _s5_core(xs, pos, mod, dskip, mats, h0, *, chain):
    nb, rows, d = xs.shape
    assert rows == ROWS and d % LANES == 0
    ns = d // LANES
    mi, ws, wo, tab = mats
    has_pos = pos is not None
    per_b = mod.shape[0] == nb and nb > 1
    in_specs = [pl.BlockSpec((None, ROWS, LANES), lambda k, b: (b, 0, k))]
    args = [xs]
    if has_pos:
        in_specs.append(pl.BlockSpec((ROWS, LANES), lambda k, b: (0, k)))
        args.append(pos)
    if per_b:
        in_specs.append(pl.BlockSpec((None, 8, LANES), lambda k, b: (b, 0, k)))
    else:
        in_specs.append(pl.BlockSpec((None, 8, LANES), lambda k, b: (0, 0, k)))
    args.append(mod)
    in_specs += [
        pl.BlockSpec((1, LANES), lambda k, b: (0, k)),
        pl.BlockSpec((SLAB_G, CK, CK), lambda k, b: (k, 0, 0)),
        pl.BlockSpec((SLAB_G, CK, 256), lambda k, b: (k, 0, 0)),
        pl.BlockSpec((SLAB_G, CK, 256), lambda k, b: (k, 0, 0)),
        pl.BlockSpec((SLAB_G, 2, TAB_ROWS, LANES), lambda k, b: (k, 0, 0, 0)),
    ]
    args += [dskip, mi, ws, wo, tab]
    out_shape = [jax.ShapeDtypeStruct((nb, ROWS, d), BF16)]
    out_specs = [pl.BlockSpec((None, ROWS, LANES), lambda k, b: (b, 0, k))]
    if chain:
        in_specs.append(pl.BlockSpec((None, None, 2 * SLAB_G, LANES), lambda k, b: (k, b, 0, 0)))
        args.append(h0)
    else:
        out_shape.append(jax.ShapeDtypeStruct((ns, nb, 2 * SLAB_G, NQ, LANES), F32))
        out_specs.append(pl.BlockSpec((None, None, 2 * SLAB_G, NQ, LANES), lambda k, b: (k, b, 0, 0, 0)))
    scratch = [
        pltpu.VMEM((ROWS, LANES), F32),
        pltpu.VMEM((SLAB_G, CHUNK, S5_CH, NJ), BF16),
        pltpu.VMEM((SLAB_G, 2, NJ, LANES), F32),
        pltpu.VMEM((SLAB_G, 2, NJ, LANES), F32),
        pltpu.VMEM((SLAB_G, CK, NJ), F32),
        pltpu.VMEM((ROWS, LANES), F32),
        pltpu.VMEM((NQ, LANES), F32),
        pltpu.VMEM((NQ, LANES), F32),
    ]
    res = pl.pallas_call(
        functools.partial(_s5_core_kernel, chain=chain, has_pos=has_pos),
        out_shape=out_shape,
        grid=(ns, nb),
        in_specs=in_specs,
        out_specs=out_specs,
        scratch_shapes=scratch,
        compiler_params=pltpu.CompilerParams(vmem_limit_bytes=VMEM_LIMIT),
        name="s5_core_lat" if chain else "s5_core_ctx",
    )(*args)
    return res if not chain else res[0]


def _cpow(ar, ai, n):
    pr = [jnp.ones_like(ar)]
    pi = [jnp.zeros_like(ai)]
    for _ in range(n):
        r, i = pr[-1], pi[-1]
        pr.append(r * ar - i * ai)
        pi.append(r * ai + i * ar)
    return jnp.stack(pr), jnp.stack(pi)


def _s5_matrices(lam_re, lam_im, log_dt, b_re, b_im, c_re, c_im):
    lr = jnp.minimum(lam_re.astype(F32), LAMBDA_RE_MAX)
    li = lam_im.astype(F32)
    dt = jnp.exp(log_dt.astype(F32))[..., None]
    mag = jnp.exp(lr * dt)
    ab_re = mag * jnp.cos(li * dt)
    ab_im = mag * jnp.sin(li * dt)
    den = lr * lr + li * li
    nr = ab_re - 1.0
    ni = ab_im
    coef_re = (nr * lr + ni * li) / den
    coef_im = (ni * lr - nr * li) / den
    br = b_re.astype(F32)
    bi = b_im.astype(F32)
    bb_re = coef_re[..., None] * br - coef_im[..., None] * bi
    bb_im = coef_re[..., None] * bi + coef_im[..., None] * br
    cr = c_re.astype(F32)
    ci = c_im.astype(F32)
    g = lr.shape[1]
    p = lr.shape[2]
    assert 2 * p == LANES

    pw_re, pw_im = _cpow(ab_re, ab_im, CHUNK)
    q_re, q_im = _cpow(pw_re[CHUNK], pw_im[CHUNK], CHUNK)

    ca_re = cr[None] * pw_re[:, :, :, None, :] - ci[None] * pw_im[:, :, :, None, :]
    ca_im = cr[None] * pw_im[:, :, :, None, :] + ci[None] * pw_re[:, :, :, None, :]
    kk = (jnp.einsum("kdgcp,dgpe->kdgce", ca_re[:CHUNK], bb_re, precision=HI)
          - jnp.einsum("kdgcp,dgpe->kdgce", ca_im[:CHUNK], bb_im, precision=HI))
    tt = jnp.arange(CHUNK)[:, None]
    ss = jnp.arange(CHUNK)[None, :]
    lag_f = jnp.clip(tt - ss, 0, CHUNK - 1)
    lag_b = jnp.clip(ss - tt, 0, CHUNK - 1)
    kf = jnp.where((tt >= ss)[:, :, None, None, None], kk[lag_f, 0], 0.0)
    kb = jnp.where((ss >= tt)[:, :, None, None, None], kk[lag_b, 1], 0.0)
    mi = jnp.transpose(kf + kb, (2, 0, 3, 1, 4)).reshape(g, CK, CK)

    ab_b_re = pw_re[..., None] * bb_re[None] - pw_im[..., None] * bb_im[None]
    ab_b_im = pw_re[..., None] * bb_im[None] + pw_im[..., None] * bb_re[None]
    sidx_f = CHUNK - 1 - jnp.arange(CHUNK)
    sidx_b = jnp.arange(CHUNK)
    ws_parts = [ab_b_re[sidx_f, 0], ab_b_im[sidx_f, 0], ab_b_re[sidx_b, 1], ab_b_im[sidx_b, 1]]
    ws = jnp.concatenate([jnp.transpose(w, (1, 0, 3, 2)) for w in ws_parts], axis=-1)
    ws = ws.reshape(g, CK, 4 * p)

    tidx_f = jnp.arange(CHUNK) + 1
    tidx_b = CHUNK - jnp.arange(CHUNK)
    wo_parts = [ca_re[tidx_f, 0], -ca_im[tidx_f, 0], ca_re[tidx_b, 1], -ca_im[tidx_b, 1]]
    wo = jnp.concatenate([jnp.transpose(w, (1, 0, 2, 3)) for w in wo_parts], axis=-1)
    wo = wo.reshape(g, CK, 4 * p)

    def pair(vr, vi):
        return jnp.concatenate([vr, vr], axis=-1), jnp.concatenate([-vi, vi], axis=-1)

    a16_1, a16_2 = pair(q_re[1], q_im[1])
    a256_1, a256_2 = pair(q_re[CHUNK], q_im[CHUNK])
    pf1, pf2 = pair(q_re[:CHUNK], q_im[:CHUNK])
    pidx = jnp.stack([jnp.arange(CHUNK), CHUNK - 1 - jnp.arange(CHUNK)])
    dsel = jnp.arange(2)[:, None]
    p1 = jnp.transpose(pf1[pidx, dsel], (2, 0, 1, 3))
    p2 = jnp.transpose(pf2[pidx, dsel], (2, 0, 1, 3))
    head = jnp.stack([a16_1, a16_2, a256_1, a256_2], axis=2)
    head = jnp.transpose(head, (1, 0, 2, 3))
    tab = jnp.concatenate([head, jnp.zeros((g, 2, 4, LANES), F32), p1, p2], axis=2)
    assert tab.shape[2] == TAB_ROWS
    return mi.astype(BF16), ws.astype(BF16), wo.astype(BF16), tab


def _conv_front_kernel(x_ref, mod_ref, w1_ref, b1_ref, wdw_ref, cvp_ref, o_ref, a_ref, cv_ref,
                       *, width, tile):
    length, d = x_ref.shape
    ns = d // LANES
    half = width // 2
    pad = 16
    a_ref[:, 0:pad, :] = jnp.zeros((ns, pad, LANES), F32)
    a_ref[:, pad + length:, :] = jnp.zeros((ns, pad, LANES), F32)
    mod = mod_ref[...]
    shift = mod[0:1]
    scale1 = 1.0 + mod[1:2]

    def pw_body(i, carry):
        r0 = pl.multiple_of(i * tile, tile)
        h = (x_ref[pl.ds(r0, tile), :] * scale1 + shift).astype(BF16)
        a = jnp.dot(h, w1_ref[...], preferred_element_type=F32) + b1_ref[...]
        a = a[:, :d] * jax.nn.sigmoid(a[:, d:])
        for sl in range(ns):
            a_ref[sl, pl.ds(pad + r0, tile), :] = a[:, sl * LANES:(sl + 1) * LANES]
        return carry

    lax.fori_loop(0, length // tile, pw_body, 0)

    def dw_body(sl, carry):
        for r0 in range(0, length, tile):
            acc = jnp.zeros((tile, LANES), F32)
            for k in range(width):
                lo = r0 + pad - half + k
                acc = acc + wdw_ref[sl, k:k + 1, :] * a_ref[sl, lo:lo + tile, :]
            cv_ref[sl, r0:r0 + tile, :] = acc
        return carry

    lax.fori_loop(0, ns, dw_body, 0)

    cvp = cvp_ref[...]

    def ln_body(i, carry):
        r0 = pl.multiple_of(i * tile, tile)
        cv = jnp.concatenate([cv_ref[sl, pl.ds(r0, tile), :] for sl in range(ns)], axis=1)
        y = _ln(cv + cvp[0:1], cvp[1:2], cvp[2:3])
        o_ref[pl.ds(r0, tile), :] = (y * jax.nn.sigmoid(y)).astype(BF16)
        return carry

    lax.fori_loop(0, length // tile, ln_body, 0)


def _conv_front(xs, mod, w1, b1, wdw, cvp, *, width):
    nb, length, d = xs.shape
    ns = d // LANES
    tile = 128
    assert length % tile == 0
    per_b = mod.shape[0] == nb and nb > 1
    mod_map = (lambda b: (b, 0, 0)) if per_b else (lambda b: (0, 0, 0))
    single = pl.Buffered(1)
    return pl.pallas_call(
        functools.partial(_conv_front_kernel, width=width, tile=tile),
        out_shape=jax.ShapeDtypeStruct((nb, length, d), BF16),
        grid=(nb,),
        in_specs=[
            pl.BlockSpec((None, length, d), lambda b: (b, 0, 0)),
            pl.BlockSpec((None, 8, d), mod_map),
            pl.BlockSpec((d, 2 * d), lambda b: (0, 0), pipeline_mode=single),
            pl.BlockSpec((1, 2 * d), lambda b: (0, 0)),
            pl.BlockSpec(wdw.shape, lambda b: (0, 0, 0)),
            pl.BlockSpec((8, d), lambda b: (0, 0)),
        ],
        out_specs=pl.BlockSpec((None, length, d), lambda b: (b, 0, 0)),
        scratch_shapes=[pltpu.VMEM((ns, length + 32, LANES), F32), pltpu.VMEM((ns, length, LANES), F32)],
        compiler_params=pltpu.CompilerParams(vmem_limit_bytes=VMEM_LIMIT),
        name="conv_front",
    )(xs, mod, w1, b1, wdw, cvp)


def _tail_mlp_kernel(*refs, glu, has_pos, alpha, hidden_chunk):
    it = iter(refs)
    x_ref = next(it)
    pos_ref = next(it) if has_pos else None
    m_ref, mod_ref, wm_ref, bm_ref, ln_ref, w1_ref, w2_ref, o_ref = it
    tm, d = x_ref.shape
    x = x_ref[...]
    if has_pos:
        x = x + pos_ref[...]
    mod = mod_ref[...]
    lnp = ln_ref[...]
    zz = jnp.dot(m_ref[...], wm_ref[...], preferred_element_type=F32) + bm_ref[...]
    if glu:
        mix = zz[:, :d] * jax.nn.sigmoid(zz[:, d:])
    else:
        mix = zz
    x1 = _ln(alpha * x + mod[2:3] * mix, lnp[0:1], lnp[2:3])
    h = (x1 * (1.0 + mod[4:5]) + mod[3:4]).astype(BF16)
    dff = w1_ref.shape[1]
    acc = jnp.zeros((tm, d), F32)
    for c in range(dff // hidden_chunk):
        lo = c * hidden_chunk
        hid = jnp.dot(h, w1_ref[:, lo:lo + hidden_chunk], preferred_element_type=F32)
        hid = jnp.square(jnp.maximum(hid, 0.0)).astype(BF16)
        acc = acc + jnp.dot(hid, w2_ref[lo:lo + hidden_chunk, :], preferred_element_type=F32)
    o_ref[...] = _ln(alpha * x1 + mod[5:6] * acc, lnp[1:2], lnp[3:4])


def _tail_mlp(xs, pos, mix_in, mod, wm, bm, lnp, w1, w2, *, glu, alpha):
    nb, length, d = xs.shape
    tm = min(512, length)
    assert length % tm == 0
    has_pos = pos is not None
    per_b = mod.shape[0] == nb and nb > 1
    mod_map = (lambda b, t: (b, 0, 0)) if per_b else (lambda b, t: (0, 0, 0))
    const = lambda b, t: (0, 0)
    single = pl.Buffered(1)
    in_specs = [pl.BlockSpec((None, tm, d), lambda b, t: (b, t, 0))]
    args = [xs]
    if has_pos:
        in_specs.append(pl.BlockSpec((tm, d), lambda b, t: (t, 0)))
        args.append(pos)
    in_specs += [
        pl.BlockSpec((None, tm, d), lambda b, t: (b, t, 0)),
        pl.BlockSpec((None, 8, d), mod_map),
        pl.BlockSpec(wm.shape, const, pipeline_mode=single),
        pl.BlockSpec(bm.shape, const),
        pl.BlockSpec((8, d), const),
        pl.BlockSpec(w1.shape, const, pipeline_mode=single),
        pl.BlockSpec(w2.shape, const, pipeline_mode=single),
    ]
    args += [mix_in, mod, wm, bm, lnp, w1, w2]
    return pl.pallas_call(
        functools.partial(_tail_mlp_kernel, glu=glu, has_pos=has_pos, alpha=alpha,
                          hidden_chunk=min(1024, w1.shape[1])),
        out_shape=jax.ShapeDtypeStruct((nb, length, d), F32),
        grid=(nb, length // tm),
        in_specs=in_specs,
        out_specs=pl.BlockSpec((None, tm, d), lambda b, t: (b, t, 0)),
        compiler_params=pltpu.CompilerParams(vmem_limit_bytes=VMEM_LIMIT),
        name="tail_glu" if glu else "tail_pw",
    )(*args)


def _sincos_1d(pos, dim):
    quarter = dim // 2
    omega = POS_TEMP ** (-jnp.arange(quarter, dtype=F32) / quarter)
    ang = pos[:, None] * omega[None, :]
    return jnp.concatenate([jnp.sin(ang), jnp.cos(ang)], axis=-1)


def _grid_pos_embed(rows, dim):
    row_idx = jnp.repeat(jnp.arange(rows), GRID_W).astype(F32)
    col_idx = jnp.tile(jnp.arange(GRID_W), rows).astype(F32)
    return jnp.concatenate([_sincos_1d(row_idx, dim // 2), _sincos_1d(col_idx, dim // 2)], axis=-1)


def _rows8(*rows):
    d = rows[0].shape[-1]
    out = jnp.stack([r.astype(F32).reshape(d) for r in rows])
    return jnp.concatenate([out, jnp.zeros((8 - len(rows), d), F32)], axis=0)


def kernel(x, c, ctx, c_ctx, w_ada, b_ada, ln_gain, ln_bias, s5_lam_re, s5_lam_im, s5_log_dt, s5_b_re, s5_b_im, s5_c_re, s5_c_im, s5_d, s5_w_glu, s5_b_glu, cv_w_pw1, cv_b_pw1, cv_w_dw, cv_b_dw, cv_ln_g, cv_ln_b, cv_w_pw2, cv_b_pw2, mlp_w1, mlp_w2):
    bsz, length, d = x.shape
    lctx = ctx.shape[1]
    depth = w_ada.shape[0]
    width = cv_w_dw.shape[1]
    alpha = (2.0 * depth) ** 0.25
    mixers = ("s5", "conv")
    kinds = [mixers[i % 2] for i in range(depth)]
    assert length == ROWS and NQ * lctx == ROWS and bsz % NQ == 0 and d % LANES == 0
    nbc = bsz // NQ

    pos = _grid_pos_embed(length // GRID_W, d).astype(x.dtype)
    nrow = -(-(bsz + 1) // 8) * 8
    cc = jnp.concatenate([c.astype(F32), c_ctx.astype(F32)[None], jnp.zeros((nrow - bsz - 1, d), F32)], axis=0)
    mods = _ada_mods(cc, w_ada.astype(F32), b_ada.astype(F32))
    mods = jnp.pad(mods.reshape(depth, nrow, 6, d), ((0, 0), (0, 0), (0, 2), (0, 0)))

    s5_j = 0
    cv_j = 0
    for i, kind in enumerate(kinds):
        ctx_needed_later = any(k == "s5" for k in kinds[i + 1:])
        use_ctx = kind == "s5" or ctx_needed_later
        mod_lat = mods[i, :bsz]
        mod_ctx = mods[i, bsz:bsz + 1]
        lnp = _rows8(ln_gain[i, 0], ln_gain[i, 1], ln_bias[i, 0], ln_bias[i, 1])
        w1 = mlp_w1[i].astype(BF16)
        w2 = mlp_w2[i].astype(BF16)
        pos_i = pos if i == 0 else None
        if kind == "s5":
            j = s5_j
            s5_j += 1
            mats = _s5_matrices(s5_lam_re[j], s5_lam_im[j], s5_log_dt[j], s5_b_re[j], s5_b_im[j],
                                s5_c_re[j], s5_c_im[j])
            dskip = s5_d[j].astype(F32).reshape(1, d)
            wm = s5_w_glu[j].astype(BF16)
            bm = s5_b_glu[j].astype(F32).reshape(1, 2 * d)
            zc, hfin = _s5_core(ctx.reshape(nbc, ROWS, d), None, mod_ctx, dskip, mats, None, chain=False)
            h0 = jnp.transpose(hfin, (0, 1, 3, 2, 4)).reshape(d // LANES, bsz, 2 * SLAB_G, LANES)
            z = _s5_core(x, pos_i, mod_lat, dskip, mats, h0, chain=True)
            x_new = _tail_mlp(x, pos_i, z, mod_lat, wm, bm, lnp, w1, w2, glu=True, alpha=alpha)
            if ctx_needed_later:
                ctx = _tail_mlp(ctx, None, zc.reshape(bsz, lctx, d), mod_ctx, wm, bm, lnp, w1, w2,
                                glu=True, alpha=alpha)
            x = x_new
        else:
            j = cv_j
            cv_j += 1
            wp1 = cv_w_pw1[j].astype(BF16)
            bp1 = cv_b_pw1[j].astype(F32).reshape(1, 2 * d)
            wdw = jnp.concatenate([cv_w_dw[j].astype(F32), jnp.zeros((-width % 8, d), F32)], axis=0)
            wdw = jnp.transpose(wdw.reshape(-1, d // LANES, LANES), (1, 0, 2))
            cvp = _rows8(cv_b_dw[j], cv_ln_g[j], cv_ln_b[j])
            wm = cv_w_pw2[j].astype(BF16)
            bm = cv_b_pw2[j].astype(F32).reshape(1, d)
            a = _conv_front(x, mod_lat, wp1, bp1, wdw, cvp, width=width)
            x_new = _tail_mlp(x, pos_i, a, mod_lat, wm, bm, lnp, w1, w2, glu=False, alpha=alpha)
            if ctx_needed_later:
                ac = _conv_front(ctx, mod_ctx, wp1, bp1, wdw, cvp, width=width)
                ctx = _tail_mlp(ctx, None, ac, mod_ctx, wm, bm, lnp, w1, w2, glu=False, alpha=alpha)
            x = x_new
        del use_ctx
    return x
```

```python
import functools
import math

import jax
import jax.numpy as jnp
from jax import lax
from jax.experimental import pallas as pl
from jax.experimental.pallas import tpu as pltpu

F32 = jnp.float32
BF16 = jnp.bfloat16

LANES = 128
S5_CH = 16
CHUNK = 16
CK = CHUNK * S5_CH
SLAB_G = LANES // S5_CH
NQ = 8
NJ = NQ * CHUNK
ROWS = NJ * CHUNK
PITCH = 24
TAB_ROWS = 40
GRID_W = 64
POS_TEMP = 10000.0
LN_EPS = 1e-5
LAMBDA_RE_MAX = -1e-4
VMEM_LIMIT = 56 * 1024 * 1024
HI = lax.Precision.HIGHEST


def _ln(v, g, b):
    mu = jnp.mean(v, axis=-1, keepdims=True)
    d = v - mu
    var = jnp.mean(d * d, axis=-1, keepdims=True)
    return d * lax.rsqrt(var + LN_EPS) * g + b


def _ada_kernel(cc_ref, w_ref, b_ref, o_ref):
    cc = cc_ref[...]
    s = cc * jax.nn.sigmoid(cc)
    o_ref[...] = jnp.dot(s, w_ref[...], preferred_element_type=F32, precision=HI) + b_ref[...]


def _ada_mods(cc, w_ada, b_ada):
    depth, d, d6 = w_ada.shape
    rows = cc.shape[0]
    tn = 1536 if d6 % 1536 == 0 else d6
    return pl.pallas_call(
        _ada_kernel,
        out_shape=jax.ShapeDtypeStruct((depth, rows, d6), F32),
        grid=(depth, d6 // tn),
        in_specs=[
            pl.BlockSpec((rows, d), lambda i, j: (0, 0)),
            pl.BlockSpec((None, d, tn), lambda i, j: (i, 0, j)),
            pl.BlockSpec((None, 1, tn), lambda i, j: (i, 0, j)),
        ],
        out_specs=pl.BlockSpec((None, rows, tn), lambda i, j: (i, 0, j)),
        compiler_params=pltpu.CompilerParams(vmem_limit_bytes=VMEM_LIMIT),
        name="ada_mods",
    )(cc, w_ada, b_ada.reshape(depth, 1, d6))


def _s5_core_kernel(*refs, chain, has_pos, want_z):
    it = iter(refs)
    x_ref = next(it)
    pos_ref = next(it) if has_pos else None
    mod_ref = next(it)
    dsk_ref = next(it)
    mi_ref = next(it)
    ws_ref = next(it)
    wo_ref = next(it)
    tab_ref = next(it)
    h0_ref = next(it) if chain else None
    z_ref = next(it) if want_z else None
    hfin_ref = None if chain else next(it)
    u_ref, r_ref, s_ref, e_ref, yt_ref, y_ref, f_ref, fw_ref, e16_ref = it
    half = LANES // 2

    mod = mod_ref[...]
    scale1 = 1.0 + mod[1:2]
    shift1 = mod[0:1]
    for j in range(NJ):
        x = x_ref[j * CHUNK:(j + 1) * CHUNK, :]
        if has_pos:
            x = x + pos_ref[j * CHUNK:(j + 1) * CHUNK, :]
        u_ref[j * PITCH:j * PITCH + CHUNK, :] = x * scale1 + shift1

    for s in range(CHUNK):
        xs = u_ref[pl.ds(s, NJ, stride=PITCH), :]
        r_ref[:, s] = xs.T.reshape(SLAB_G, S5_CH, NJ).astype(BF16)

    for g in range(SLAB_G):
        rg = r_ref[g].reshape(CK, NJ)
        if want_z:
            yt_ref[g] = jnp.dot(mi_ref[g], rg, preferred_element_type=F32)
        st = lax.dot_general(rg, ws_ref[g], (((0,), (0,)), ((), ())),
                             preferred_element_type=F32)
        for q in range(NQ):
            s_ref[g, 0, q * PITCH:q * PITCH + CHUNK, :] = st[q * CHUNK:(q + 1) * CHUNK, :LANES]
            s_ref[g, 1, q * PITCH:q * PITCH + CHUNK, :] = st[q * CHUNK:(q + 1) * CHUNK, LANES:]

    def cstep(a1, a2, e, w, s, sw):
        return a1 * e + a2 * w + s, a1 * w - a2 * e + sw

    def scan_dir(g, d):
        sg = s_ref.at[g, d]
        eg = e_ref.at[g, d]
        gd = g * 2 + d
        tab = lambda r: tab_ref[r, gd:gd + 1, :]
        a1 = tab(0)
        a2 = tab(1)
        e = jnp.zeros((NQ, LANES), F32)
        w = e
        order = range(CHUNK) if d == 0 else range(CHUNK - 1, -1, -1)
        eloc = []
        for jp in order:
            eloc.append((jp, e))
            s = sg[pl.ds(jp, NQ, stride=PITCH), :]
            e, w = cstep(a1, a2, e, w, s, pltpu.roll(s, half, 1))
        if not chain:
            if want_z:
                for jp, el in eloc:
                    eg[pl.ds(jp, NQ, stride=PITCH), :] = el
            hfin_ref[gd] = e
            return
        f_ref[gd] = e
        fw_ref[gd] = w
        b1 = tab(2)
        b2 = tab(3)
        cst = jnp.broadcast_to(h0_ref[gd:gd + 1, :], (NQ, LANES))
        cstw = pltpu.roll(cst, half, 1)
        qs = range(NQ) if d == 0 else range(NQ - 1, -1, -1)
        for q in qs:
            e16_ref[gd, q:q + 1, :] = cst[0:1, :]
            cst, cstw = cstep(b1, b2, cst, cstw, f_ref[gd, q:q + 1, :], fw_ref[gd, q:q + 1, :])
        e16 = e16_ref[gd]
        e16r = pltpu.roll(e16, half, 1)
        for jp, el in eloc:
            eg[pl.ds(jp, NQ, stride=PITCH), :] = el + tab(8 + jp) * e16 + tab(8 + CHUNK + jp) * e16r

    for g in range(SLAB_G):
        scan_dir(g, 0)
        scan_dir(g, 1)

    if not want_z:
        return

    for g in range(SLAB_G):
        ef = jnp.concatenate([e_ref[g, 0, q * PITCH:q * PITCH + CHUNK, :] for q in range(NQ)], axis=0)
        eb = jnp.concatenate([e_ref[g, 1, q * PITCH:q * PITCH + CHUNK, :] for q in range(NQ)], axis=0)
        et = jnp.concatenate([ef, eb], axis=1).astype(BF16)
        yo = lax.dot_general(wo_ref[g], et, (((1,), (1,)), ((), ())),
                             preferred_element_type=F32)
        yt_ref[g] = yt_ref[g] + yo

    for t in range(CHUNK):
        zt = yt_ref[:, t * S5_CH:(t + 1) * S5_CH, :].reshape(LANES, NJ)
        y_ref[pl.ds(t, NJ, stride=PITCH), :] = zt.T

    dsk = dsk_ref[...]
    for j in range(NJ):
        y = y_ref[j * PITCH:j * PITCH + CHUNK, :] + dsk * u_ref[j * PITCH:j * PITCH + CHUNK, :]
        z_ref[j * CHUNK:(j + 1) * CHUNK, :] = (0.5 * y * (1.0 + lax.erf(y * (1.0 / math.sqrt(2.0))))).astype(BF16)


def _s5_core(xs, pos, mod, dskip, mats, h0, *, chain, want_z=True):
    nb, rows, d = xs.shape
    assert rows == ROWS and d % LANES == 0 and (want_z or not chain)
    ns = d // LANES
    mi, ws, wo, tab = mats
    has_pos = pos is not None
    per_b = mod.shape[0] == nb and nb > 1
    in_specs = [pl.BlockSpec((None, ROWS, LANES), lambda k, b: (b, 0, k))]
    args = [xs]
    if has_pos:
        in_specs.append(pl.BlockSpec((ROWS, LANES), lambda k, b: (0, k)))
        args.append(pos)
    if per_b:
        in_specs.append(pl.BlockSpec((None, 8, LANES), lambda k, b: (b, 0, k)))
    else:
        in_specs.append(pl.BlockSpec((None, 8, LANES), lambda k, b: (0, 0, k)))
    args.append(mod)
    in_specs += [
        pl.BlockSpec((1, LANES), lambda k, b: (0, k)),
        pl.BlockSpec((SLAB_G, CK, CK), lambda k, b: (k, 0, 0)),
        pl.BlockSpec((SLAB_G, CK, 256), lambda k, b: (k, 0, 0)),
        pl.BlockSpec((SLAB_G, CK, 256), lambda k, b: (k, 0, 0)),
        pl.BlockSpec((None, TAB_ROWS, 2 * SLAB_G, LANES), lambda k, b: (k, 0, 0, 0)),
    ]
    args += [dskip, mi, ws, wo, tab]
    out_shape = []
    out_specs = []
    if want_z:
        out_shape.append(jax.ShapeDtypeStruct((nb, ROWS, d), BF16))
        out_specs.append(pl.BlockSpec((None, ROWS, LANES), lambda k, b: (b, 0, k)))
    if chain:
        in_specs.append(pl.BlockSpec((None, None, 2 * SLAB_G, LANES), lambda k, b: (k, b, 0, 0)))
        args.append(h0)
    else:
        out_shape.append(jax.ShapeDtypeStruct((ns, nb, 2 * SLAB_G, NQ, LANES), F32))
        out_specs.append(pl.BlockSpec((None, None, 2 * SLAB_G, NQ, LANES), lambda k, b: (k, b, 0, 0, 0)))
    scratch = [
        pltpu.VMEM((NJ * PITCH, LANES), F32),
        pltpu.VMEM((SLAB_G, CHUNK, S5_CH, NJ), BF16),
        pltpu.VMEM((SLAB_G, 2, NQ * PITCH, LANES), F32),
        pltpu.VMEM((SLAB_G, 2, NQ * PITCH, LANES), F32),
        pltpu.VMEM((SLAB_G, CK, NJ), F32),
        pltpu.VMEM((NJ * PITCH, LANES), F32),
        pltpu.VMEM((2 * SLAB_G, NQ, LANES), F32),
        pltpu.VMEM((2 * SLAB_G, NQ, LANES), F32),
        pltpu.VMEM((2 * SLAB_G, NQ, LANES), F32),
    ]
    res = pl.pallas_call(
        functools.partial(_s5_core_kernel, chain=chain, has_pos=has_pos, want_z=want_z),
        out_shape=out_shape,
        grid=(ns, nb),
        in_specs=in_specs,
        out_specs=out_specs,
        scratch_shapes=scratch,
        compiler_params=pltpu.CompilerParams(vmem_limit_bytes=VMEM_LIMIT),
        name="s5_core_lat" if chain else ("s5_core_ctx" if want_z else "s5_state_ctx"),
    )(*args)
    if chain:
        return res[0]
    return (res[0], res[1]) if want_z else (None, res[0])


def _s5_prep_kernel(lr_ref, li_ref, ldt_ref, bri_ref, bir_ref, cri_ref, cir_ref,
                    mi_ref, ws_ref, wo_ref, tab_ref, t1_ref, t2_ref, u1_ref, u2_ref, x_ref, cf_ref):
    nrow = 2 * SLAB_G
    half = LANES // 2
    lane = lax.broadcasted_iota(jnp.int32, (nrow, LANES), 1)
    first = lane < half
    fwd_row = lax.broadcasted_iota(jnp.int32, (nrow, LANES), 0) % 2 == 0

    lr = jnp.minimum(lr_ref[...], LAMBDA_RE_MAX)
    li = li_ref[...]
    dt = jnp.exp(ldt_ref[...])
    mag = jnp.exp(lr * dt)
    ar = mag * jnp.cos(li * dt)
    ai = mag * jnp.sin(li * dt)
    den = lr * lr + li * li
    nr = ar - 1.0
    coef_re = (nr * lr + ai * li) / den
    coef_im = (ai * lr - nr * li) / den
    cf_ref[0] = coef_re
    cf_ref[1] = jnp.where(first, -coef_im, coef_im)

    def powers(a_re, a_im):
        a1 = a_re
        a2 = jnp.where(first, -a_im, a_im)
        p = jnp.where(first, 1.0, 0.0).astype(F32)
        w = jnp.where(first, 0.0, 1.0).astype(F32)
        out = [(p, w)]
        for _ in range(CHUNK):
            p, w = a1 * p + a2 * w, a1 * w - a2 * p
            out.append((p, w))
        return out

    pw = powers(ar, ai)
    for k, (p, w) in enumerate(pw):
        t1_ref[k] = jnp.where(first, p, w)
        t2_ref[k] = jnp.where(first, -w, p)
        u1_ref[k] = jnp.where(first, p, -w)
        u2_ref[k] = jnp.where(first, -w, -p)
    p16, w16 = pw[CHUNK]
    a16_re = jnp.where(first, p16, w16)
    a16_im = jnp.where(first, w16, p16)
    qw = powers(a16_re, a16_im)
    q1 = [jnp.where(first, p, w) for p, w in qw]
    q2 = [jnp.where(first, -w, p) for p, w in qw]
    tab_ref[0] = q1[1]
    tab_ref[1] = q2[1]
    tab_ref[2] = q1[CHUNK]
    tab_ref[3] = q2[CHUNK]
    for r in range(4, 8):
        tab_ref[r] = jnp.zeros((nrow, LANES), F32)
    for jp in range(CHUNK):
        tab_ref[8 + jp] = jnp.where(fwd_row, q1[jp], q1[CHUNK - 1 - jp])
        tab_ref[8 + CHUNK + jp] = jnp.where(fwd_row, q2[jp], q2[CHUNK - 1 - jp])

    lane2 = lax.broadcasted_iota(jnp.int32, (S5_CH, CK), 1)

    def group_body(g, carry):
        mt = [None] * CHUNK
        for d in range(2):
            r = g * 2 + d
            row = lambda v: jnp.broadcast_to(v, (S5_CH, LANES))
            bri = bri_ref[g, d]
            bir = bir_ref[g, d]
            cc1 = row(cf_ref[0, pl.ds(r, 1), :])
            cc2 = row(cf_ref[1, pl.ds(r, 1), :])
            bb_ri = cc1 * bri + cc2 * bir
            bb_ir = cc1 * bir - cc2 * bri
            cri = cri_ref[g, d]
            cir = cir_ref[g, d]
            for s in range(CHUNK):
                k = CHUNK - 1 - s if d == 0 else s
                xk = row(t1_ref[k, pl.ds(r, 1), :]) * bb_ri + row(t2_ref[k, pl.ds(r, 1), :]) * bb_ir
                x_ref[s * S5_CH:(s + 1) * S5_CH, :] = xk
                ws_ref[g, s * S5_CH:(s + 1) * S5_CH, d * LANES:(d + 1) * LANES] = xk.astype(BF16)
            for t in range(CHUNK):
                k = t + 1 if d == 0 else CHUNK - t
                wk = row(u1_ref[k, pl.ds(r, 1), :]) * cri + row(u2_ref[k, pl.ds(r, 1), :]) * cir
                wo_ref[g, t * S5_CH:(t + 1) * S5_CH, d * LANES:(d + 1) * LANES] = wk.astype(BF16)
            cneg = jnp.where(first[:S5_CH], cri, -cri)
            krow = lax.dot_general(cneg, x_ref[...], (((1,), (1,)), ((), ())),
                                   preferred_element_type=F32, precision=HI)
            for t in range(CHUNK):
                if d == 0:
                    sh = (CHUNK - 1 - t) * S5_CH
                    rolled = pltpu.roll(krow, (CK - sh) % CK, 1) if sh else krow
                    mt[t] = jnp.where(lane2 < (t + 1) * S5_CH, rolled, 0.0)
                else:
                    sh = t * S5_CH
                    rolled = pltpu.roll(krow, sh, 1) if sh else krow
                    mt[t] = mt[t] + jnp.where(lane2 >= sh, rolled, 0.0)
        for t in range(CHUNK):
            mi_ref[g, t * S5_CH:(t + 1) * S5_CH, :] = mt[t].astype(BF16)
        return carry

    lax.fori_loop(0, SLAB_G, group_body, 0)


def _s5_matrices(lam_re, lam_im, log_dt, b_re, b_im, c_re, c_im):
    _, g, p = lam_re.shape
    assert 2 * p == LANES and g % SLAB_G == 0
    ns = g // SLAB_G
    nrow = 2 * SLAB_G

    def rows(v):
        v = jnp.transpose(v.astype(F32), (1, 0, 2)).reshape(ns, nrow, p)
        return jnp.concatenate([v, v], axis=-1)

    def pairs(vr, vi):
        return jnp.transpose(jnp.concatenate([vr.astype(F32), vi.astype(F32)], axis=-1), (1, 0, 2, 3))

    ldt = jnp.broadcast_to(log_dt.astype(F32)[..., None], (2, g, p))
    b_re_t = jnp.swapaxes(b_re, -1, -2)
    b_im_t = jnp.swapaxes(b_im, -1, -2)
    blk4 = pl.BlockSpec((SLAB_G, 2, S5_CH, LANES), lambda k: (k, 0, 0, 0))
    blk3 = pl.BlockSpec((None, nrow, LANES), lambda k: (k, 0, 0))
    mat = jax.ShapeDtypeStruct((g, CK, CK), BF16)
    mat_spec = pl.BlockSpec((SLAB_G, CK, CK), lambda k: (k, 0, 0))
    return pl.pallas_call(
        _s5_prep_kernel,
        out_shape=[mat, mat, mat, jax.ShapeDtypeStruct((ns, TAB_ROWS, nrow, LANES), F32)],
        grid=(ns,),
        in_specs=[blk3, blk3, blk3, blk4, blk4, blk4, blk4],
        out_specs=[mat_spec, mat_spec, mat_spec,
                   pl.BlockSpec((None, TAB_ROWS, nrow, LANES), lambda k: (k, 0, 0, 0))],
        scratch_shapes=[pltpu.VMEM((CHUNK + 1, nrow, LANES), F32)] * 4
        + [pltpu.VMEM((CK, LANES), F32), pltpu.VMEM((2, nrow, LANES), F32)],
        compiler_params=pltpu.CompilerParams(vmem_limit_bytes=VMEM_LIMIT),
        name="s5_prep",
    )(rows(lam_re), rows(lam_im), rows(ldt), pairs(b_re_t, b_im_t), pairs(b_im_t, b_re_t),
      pairs(c_re, c_im), pairs(c_im, c_re))


def _conv_front_kernel(x_ref, mod_ref, w1_ref, b1_ref, wdw_ref, cvp_ref, o_ref, a_ref, cv_ref,
                       *, width, tile):
    length, d = x_ref.shape
    ns = d // LANES
    half = width // 2
    pad = 16
    a_ref[:, 0:pad, :] = jnp.zeros((ns, pad, LANES), F32)
    a_ref[:, pad + length:, :] = jnp.zeros((ns, pad, LANES), F32)
    mod = mod_ref[...]
    shift = mod[0:1]
    scale1 = 1.0 + mod[1:2]

    def pw_body(i, carry):
        r0 = pl.multiple_of(i * tile, tile)
        h = (x_ref[pl.ds(r0, tile), :] * scale1 + shift).astype(BF16)
        a = jnp.dot(h, w1_ref[...], preferred_element_type=F32) + b1_ref[...]
        a = a[:, :d] * jax.nn.sigmoid(a[:, d:])
        for sl in range(ns):
            a_ref[sl, pl.ds(pad + r0, tile), :] = a[:, sl * LANES:(sl + 1) * LANES]
        return carry

    lax.fori_loop(0, length // tile, pw_body, 0)

    def dw_body(sl, carry):
        for r0 in range(0, length, tile):
            acc = jnp.zeros((tile, LANES), F32)
            for k in range(width):
                lo = r0 + pad - half + k
                acc = acc + wdw_ref[sl, k:k + 1, :] * a_ref[sl, lo:lo + tile, :]
            cv_ref[sl, r0:r0 + tile, :] = acc
        return carry

    lax.fori_loop(0, ns, dw_body, 0)

    cvp = cvp_ref[...]

    def ln_body(i, carry):
        r0 = pl.multiple_of(i * tile, tile)
        cv = jnp.concatenate([cv_ref[sl, pl.ds(r0, tile), :] for sl in range(ns)], axis=1)
        y = _ln(cv + cvp[0:1], cvp[1:2], cvp[2:3])
        o_ref[pl.ds(r0, tile), :] = (y * jax.nn.sigmoid(y)).astype(BF16)
        return carry

    lax.fori_loop(0, length // tile, ln_body, 0)


def _conv_front(xs, mod, w1, b1, wdw, cvp, *, width):
    nb, length, d = xs.shape
    ns = d // LANES
    tile = 128
    assert length % tile == 0
    per_b = mod.shape[0] == nb and nb > 1
    mod_map = (lambda b: (b, 0, 0)) if per_b else (lambda b: (0, 0, 0))
    single = pl.Buffered(1)
    return pl.pallas_call(
        functools.partial(_conv_front_kernel, width=width, tile=tile),
        out_shape=jax.ShapeDtypeStruct((nb, length, d), BF16),
        grid=(nb,),
        in_specs=[
            pl.BlockSpec((None, length, d), lambda b: (b, 0, 0)),
            pl.BlockSpec((None, 8, d), mod_map),
            pl.BlockSpec((d, 2 * d), lambda b: (0, 0), pipeline_mode=single),
            pl.BlockSpec((1, 2 * d), lambda b: (0, 0)),
            pl.BlockSpec(wdw.shape, lambda b: (0, 0, 0)),
            pl.BlockSpec((8, d), lambda b: (0, 0)),
        ],
        out_specs=pl.BlockSpec((None, length, d), lambda b: (b, 0, 0)),
        scratch_shapes=[pltpu.VMEM((ns, length + 32, LANES), F32), pltpu.VMEM((ns, length, LANES), F32)],
        compiler_params=pltpu.CompilerParams(vmem_limit_bytes=VMEM_LIMIT),
        name="conv_front",
    )(xs, mod, w1, b1, wdw, cvp)


def _tail_mlp_kernel(*refs, glu, has_pos, alpha, hidden_chunk):
    it = iter(refs)
    x_ref = next(it)
    pos_ref = next(it) if has_pos else None
    m_ref, mod_ref, wm_ref, bm_ref, ln_ref, w1_ref, w2_ref, o_ref = it
    tm, d = x_ref.shape
    x = x_ref[...]
    if has_pos:
        x = x + pos_ref[...]
    mod = mod_ref[...]
    lnp = ln_ref[...]
    zz = jnp.dot(m_ref[...], wm_ref[...], preferred_element_type=F32) + bm_ref[...]
    if glu:
        mix = zz[:, :d] * jax.nn.sigmoid(zz[:, d:])
    else:
        mix = zz
    x1 = _ln(alpha * x + mod[2:3] * mix, lnp[0:1], lnp[2:3])
    h = (x1 * (1.0 + mod[4:5]) + mod[3:4]).astype(BF16)
    dff = w1_ref.shape[1]
    acc = jnp.zeros((tm, d), F32)
    for c in range(dff // hidden_chunk):
        lo = c * hidden_chunk
        hid = jnp.dot(h, w1_ref[:, lo:lo + hidden_chunk], preferred_element_type=F32)
        hid = jnp.square(jnp.maximum(hid, 0.0)).astype(BF16)
        acc = acc + jnp.dot(hid, w2_ref[lo:lo + hidden_chunk, :], preferred_element_type=F32)
    o_ref[...] = _ln(alpha * x1 + mod[5:6] * acc, lnp[1:2], lnp[3:4])


def _tail_mlp(xs, pos, mix_in, mod, wm, bm, lnp, w1, w2, *, glu, alpha):
    nb, length, d = xs.shape
    tm = min(512, length)
    assert length % tm == 0
    has_pos = pos is not None
    per_b = mod.shape[0] == nb and nb > 1
    mod_map = (lambda b, t: (b, 0, 0)) if per_b else (lambda b, t: (0, 0, 0))
    const = lambda b, t: (0, 0)
    single = pl.Buffered(1)
    in_specs = [pl.BlockSpec((None, tm, d), lambda b, t: (b, t, 0))]
    args = [xs]
    if has_pos:
        in_specs.append(pl.BlockSpec((tm, d), lambda b, t: (t, 0)))
        args.append(pos)
    in_specs += [
        pl.BlockSpec((None, tm, d), lambda b, t: (b, t, 0)),
        pl.BlockSpec((None, 8, d), mod_map),
        pl.BlockSpec(wm.shape, const, pipeline_mode=single),
        pl.BlockSpec(bm.shape, const),
        pl.BlockSpec((8, d), const),
        pl.BlockSpec(w1.shape, const, pipeline_mode=single),
        pl.BlockSpec(w2.shape, const, pipeline_mode=single),
    ]
    args += [mix_in, mod, wm, bm, lnp, w1, w2]
    return pl.pallas_call(
        functools.partial(_tail_mlp_kernel, glu=glu, has_pos=has_pos, alpha=alpha,
                          hidden_chunk=min(1024, w1.shape[1])),
        out_shape=jax.ShapeDtypeStruct((nb, length, d), F32),
        grid=(nb, length // tm),
        in_specs=in_specs,
        out_specs=pl.BlockSpec((None, tm, d), lambda b, t: (b, t, 0)),
        compiler_params=pltpu.CompilerParams(vmem_limit_bytes=VMEM_LIMIT),
        name="tail_glu" if glu else "tail_pw",
    )(*args)


def _sincos_1d(pos, dim):
    quarter = dim // 2
    omega = POS_TEMP ** (-jnp.arange(quarter, dtype=F32) / quarter)
    ang = pos[:, None] * omega[None, :]
    return jnp.concatenate([jnp.sin(ang), jnp.cos(ang)], axis=-1)


def _grid_pos_embed(rows, dim):
    row_idx = jnp.repeat(jnp.arange(rows), GRID_W).astype(F32)
    col_idx = jnp.tile(jnp.arange(GRID_W), rows).astype(F32)
    return jnp.concatenate([_sincos_1d(row_idx, dim // 2), _sincos_1d(col_idx, dim // 2)], axis=-1)


def _rows8(*rows):
    d = rows[0].shape[-1]
    out = jnp.stack([r.astype(F32).reshape(d) for r in rows])
    return jnp.concatenate([out, jnp.zeros((8 - len(rows), d), F32)], axis=0)


def kernel(x, c, ctx, c_ctx, w_ada, b_ada, ln_gain, ln_bias, s5_lam_re, s5_lam_im, s5_log_dt, s5_b_re, s5_b_im, s5_c_re, s5_c_im, s5_d, s5_w_glu, s5_b_glu, cv_w_pw1, cv_b_pw1, cv_w_dw, cv_b_dw, cv_ln_g, cv_ln_b, cv_w_pw2, cv_b_pw2, mlp_w1, mlp_w2):
    bsz, length, d = x.shape
    lctx = ctx.shape[1]
    depth = w_ada.shape[0]
    width = cv_w_dw.shape[1]
    alpha = (2.0 * depth) ** 0.25
    mixers = ("s5", "conv")
    kinds = [mixers[i % 2] for i in range(depth)]
    assert length == ROWS and NQ * lctx == ROWS and bsz % NQ == 0 and d % LANES == 0
    nbc = bsz // NQ

    pos = _grid_pos_embed(length // GRID_W, d).astype(x.dtype)
    nrow = -(-(bsz + 1) // 8) * 8
    cc = jnp.concatenate([c.astype(F32), c_ctx.astype(F32)[None], jnp.zeros((nrow - bsz - 1, d), F32)], axis=0)
    mods = _ada_mods(cc, w_ada.astype(F32), b_ada.astype(F32))
    mods = jnp.pad(mods.reshape(depth, nrow, 6, d), ((0, 0), (0, 0), (0, 2), (0, 0)))

    s5_j = 0
    cv_j = 0
    for i, kind in enumerate(kinds):
        ctx_needed_later = any(k == "s5" for k in kinds[i + 1:])
        use_ctx = kind == "s5" or ctx_needed_later
        mod_lat = mods[i, :bsz]
        mod_ctx = mods[i, bsz:bsz + 1]
        lnp = _rows8(ln_gain[i, 0], ln_gain[i, 1], ln_bias[i, 0], ln_bias[i, 1])
        w1 = mlp_w1[i].astype(BF16)
        w2 = mlp_w2[i].astype(BF16)
        pos_i = pos if i == 0 else None
        if kind == "s5":
            j = s5_j
            s5_j += 1
            mats = _s5_matrices(s5_lam_re[j], s5_lam_im[j], s5_log_dt[j], s5_b_re[j], s5_b_im[j],
                                s5_c_re[j], s5_c_im[j])
            dskip = s5_d[j].astype(F32).reshape(1, d)
            wm = s5_w_glu[j].astype(BF16)
            bm = s5_b_glu[j].astype(F32).reshape(1, 2 * d)
            zc, hfin = _s5_core(ctx.reshape(nbc, ROWS, d), None, mod_ctx, dskip, mats, None, chain=False,
                                want_z=ctx_needed_later)
            h0 = jnp.transpose(hfin, (0, 1, 3, 2, 4)).reshape(d // LANES, bsz, 2 * SLAB_G, LANES)
            z = _s5_core(x, pos_i, mod_lat, dskip, mats, h0, chain=True)
            x_new = _tail_mlp(x, pos_i, z, mod_lat, wm, bm, lnp, w1, w2, glu=True, alpha=alpha)
            if ctx_needed_later:
                ctx = _tail_mlp(ctx, None, zc.reshape(bsz, lctx, d), mod_ctx, wm, bm, lnp, w1, w2,
                                glu=True, alpha=alpha)
            x = x_new
        else:
            j = cv_j
            cv_j += 1
            wp1 = cv_w_pw1[j].astype(BF16)
            bp1 = cv_b_pw1[j].astype(F32).reshape(1, 2 * d)
            wdw = jnp.concatenate([cv_w_dw[j].astype(F32), jnp.zeros((-width % 8, d), F32)], axis=0)
            wdw = jnp.transpose(wdw.reshape(-1, d // LANES, LANES), (1, 0, 2))
            cvp = _rows8(cv_b_dw[j], cv_ln_g[j], cv_ln_b[j])
            wm = cv_w_pw2[j].astype(BF16)
            bm = cv_b_pw2[j].astype(F32).reshape(1, d)
            a = _conv_front(x, mod_lat, wp1, bp1, wdw, cvp, width=width)
            x_new = _tail_mlp(x, pos_i, a, mod_lat, wm, bm, lnp, w1, w2, glu=False, alpha=alpha)
            if ctx_needed_later:
                ac = _conv_front(ctx, mod_ctx, wp1, bp1, wdw, cvp, width=width)
                ctx = _tail_mlp(ctx, None, ac, mod_ctx, wm, bm, lnp, w1, w2, glu=False, alpha=alpha)
            x = x_new
        del use_ctx
    return x
```

```python
import functools
import math

import jax
import jax.numpy as jnp
from jax import lax
from jax.experimental import pallas as pl
from jax.experimental.pallas import tpu as pltpu

F32 = jnp.float32
BF16 = jnp.bfloat16

LANES = 128
S5_CH = 16
CHUNK = 16
CK = CHUNK * S5_CH
SLAB_G = LANES // S5_CH
NQ = 8
NJ = NQ * CHUNK
ROWS = NJ * CHUNK
PITCH = 24
TAB_ROWS = 40
HALO = 16
GRID_W = 64
POS_TEMP = 10000.0
LN_EPS = 1e-5
LAMBDA_RE_MAX = -1e-4
VMEM_LIMIT = 56 * 1024 * 1024
HI = lax.Precision.HIGHEST


def _ln(v, g, b):
    mu = jnp.mean(v, axis=-1, keepdims=True)
    d = v - mu
    var = jnp.mean(d * d, axis=-1, keepdims=True)
    return d * lax.rsqrt(var + LN_EPS) * g + b


def _ada_kernel(cc_ref, w_ref, b_ref, o_ref):
    cc = cc_ref[...]
    s = cc * jax.nn.sigmoid(cc)
    o_ref[...] = jnp.dot(s, w_ref[...], preferred_element_type=F32, precision=HI) + b_ref[...]


def _ada_mods(cc, w_ada, b_ada):
    depth, d, d6 = w_ada.shape
    rows = cc.shape[0]
    tn = 1536 if d6 % 1536 == 0 else d6
    return pl.pallas_call(
        _ada_kernel,
        out_shape=jax.ShapeDtypeStruct((depth, rows, d6), F32),
        grid=(depth, d6 // tn),
        in_specs=[
            pl.BlockSpec((rows, d), lambda i, j: (0, 0)),
            pl.BlockSpec((None, d, tn), lambda i, j: (i, 0, j)),
            pl.BlockSpec((None, 1, tn), lambda i, j: (i, 0, j)),
        ],
        out_specs=pl.BlockSpec((None, rows, tn), lambda i, j: (i, 0, j)),
        compiler_params=pltpu.CompilerParams(vmem_limit_bytes=VMEM_LIMIT),
        name="ada_mods",
    )(cc, w_ada, b_ada.reshape(depth, 1, d6))


def _s5_core_kernel(*refs, chain, has_pos, want_z):
    it = iter(refs)
    x_ref = next(it)
    pos_ref = next(it) if has_pos else None
    mod_ref = next(it)
    dsk_ref = next(it)
    mi_ref = next(it)
    ws_ref = next(it)
    wo_ref = next(it)
    tab_ref = next(it)
    h0_ref = next(it) if chain else None
    z_ref = next(it) if want_z else None
    hfin_ref = None if chain else next(it)
    u_ref, r_ref, s_ref, e_ref, yt_ref, y_ref, f_ref, fw_ref, e16_ref = it
    half = LANES // 2

    mod = mod_ref[...]
    scale1 = 1.0 + mod[1:2]
    shift1 = mod[0:1]
    for j in range(NJ):
        x = x_ref[j * CHUNK:(j + 1) * CHUNK, :]
        if has_pos:
            x = x + pos_ref[j * CHUNK:(j + 1) * CHUNK, :]
        u_ref[j * PITCH:j * PITCH + CHUNK, :] = x * scale1 + shift1

    for s in range(CHUNK):
        xs = u_ref[pl.ds(s, NJ, stride=PITCH), :]
        r_ref[:, s] = xs.T.reshape(SLAB_G, S5_CH, NJ).astype(BF16)

    for g in range(SLAB_G):
        rg = r_ref[g].reshape(CK, NJ)
        if want_z:
            yt_ref[g] = jnp.dot(mi_ref[g], rg, preferred_element_type=F32)
        st = lax.dot_general(rg, ws_ref[g], (((0,), (0,)), ((), ())),
                             preferred_element_type=F32)
        for q in range(NQ):
            s_ref[g, 0, q * PITCH:q * PITCH + CHUNK, :] = st[q * CHUNK:(q + 1) * CHUNK, :LANES]
            s_ref[g, 1, q * PITCH:q * PITCH + CHUNK, :] = st[q * CHUNK:(q + 1) * CHUNK, LANES:]

    def cstep(a1, a2, e, w, s, sw):
        return a1 * e + a2 * w + s, a1 * w - a2 * e + sw

    def scan_dir(g, d):
        sg = s_ref.at[g, d]
        eg = e_ref.at[g, d]
        gd = g * 2 + d
        tab = lambda r: tab_ref[r, gd:gd + 1, :]
        a1 = tab(0)
        a2 = tab(1)
        e = jnp.zeros((NQ, LANES), F32)
        w = e
        order = range(CHUNK) if d == 0 else range(CHUNK - 1, -1, -1)
        eloc = []
        for jp in order:
            eloc.append((jp, e))
            s = sg[pl.ds(jp, NQ, stride=PITCH), :]
            e, w = cstep(a1, a2, e, w, s, pltpu.roll(s, half, 1))
        if not chain:
            if want_z:
                for jp, el in eloc:
                    eg[pl.ds(jp, NQ, stride=PITCH), :] = el
            hfin_ref[gd] = e
            return
        f_ref[gd] = e
        fw_ref[gd] = w
        b1 = tab(2)
        b2 = tab(3)
        cst = jnp.broadcast_to(h0_ref[gd:gd + 1, :], (NQ, LANES))
        cstw = pltpu.roll(cst, half, 1)
        qs = range(NQ) if d == 0 else range(NQ - 1, -1, -1)
        for q in qs:
            e16_ref[gd, q:q + 1, :] = cst[0:1, :]
            cst, cstw = cstep(b1, b2, cst, cstw, f_ref[gd, q:q + 1, :], fw_ref[gd, q:q + 1, :])
        e16 = e16_ref[gd]
        e16r = pltpu.roll(e16, half, 1)
        for jp, el in eloc:
            eg[pl.ds(jp, NQ, stride=PITCH), :] = el + tab(8 + jp) * e16 + tab(8 + CHUNK + jp) * e16r

    for g in range(SLAB_G):
        scan_dir(g, 0)
        scan_dir(g, 1)

    if not want_z:
        return

    for g in range(SLAB_G):
        ef = jnp.concatenate([e_ref[g, 0, q * PITCH:q * PITCH + CHUNK, :] for q in range(NQ)], axis=0)
        eb = jnp.concatenate([e_ref[g, 1, q * PITCH:q * PITCH + CHUNK, :] for q in range(NQ)], axis=0)
        et = jnp.concatenate([ef, eb], axis=1).astype(BF16)
        yo = lax.dot_general(wo_ref[g], et, (((1,), (1,)), ((), ())),
                             preferred_element_type=F32)
        yt_ref[g] = yt_ref[g] + yo

    for t in range(CHUNK):
        zt = yt_ref[:, t * S5_CH:(t + 1) * S5_CH, :].reshape(LANES, NJ)
        y_ref[pl.ds(t, NJ, stride=PITCH), :] = zt.T

    dsk = dsk_ref[...]
    for j in range(NJ):
        y = y_ref[j * PITCH:j * PITCH + CHUNK, :] + dsk * u_ref[j * PITCH:j * PITCH + CHUNK, :]
        z_ref[j * CHUNK:(j + 1) * CHUNK, :] = (0.5 * y * (1.0 + lax.erf(y * (1.0 / math.sqrt(2.0))))).astype(BF16)


def _s5_core(xs, pos, mod, dskip, mats, h0, *, chain, want_z=True):
    nb, rows, d = xs.shape
    assert rows == ROWS and d % LANES == 0 and (want_z or not chain)
    ns = d // LANES
    mi, ws, wo, tab = mats
    has_pos = pos is not None
    per_b = mod.shape[0] == nb and nb > 1
    in_specs = [pl.BlockSpec((None, ROWS, LANES), lambda k, b: (b, 0, k))]
    args = [xs]
    if has_pos:
        in_specs.append(pl.BlockSpec((ROWS, LANES), lambda k, b: (0, k)))
        args.append(pos)
    if per_b:
        in_specs.append(pl.BlockSpec((None, 8, LANES), lambda k, b: (b, 0, k)))
    else:
        in_specs.append(pl.BlockSpec((None, 8, LANES), lambda k, b: (0, 0, k)))
    args.append(mod)
    in_specs += [
        pl.BlockSpec((1, LANES), lambda k, b: (0, k)),
        pl.BlockSpec((SLAB_G, CK, CK), lambda k, b: (k, 0, 0)),
        pl.BlockSpec((SLAB_G, CK, 256), lambda k, b: (k, 0, 0)),
        pl.BlockSpec((SLAB_G, CK, 256), lambda k, b: (k, 0, 0)),
        pl.BlockSpec((None, TAB_ROWS, 2 * SLAB_G, LANES), lambda k, b: (k, 0, 0, 0)),
    ]
    args += [dskip, mi, ws, wo, tab]
    out_shape = []
    out_specs = []
    if want_z:
        out_shape.append(jax.ShapeDtypeStruct((nb, ROWS, d), BF16))
        out_specs.append(pl.BlockSpec((None, ROWS, LANES), lambda k, b: (b, 0, k)))
    if chain:
        in_specs.append(pl.BlockSpec((None, None, 2 * SLAB_G, LANES), lambda k, b: (k, b, 0, 0)))
        args.append(h0)
    else:
        out_shape.append(jax.ShapeDtypeStruct((ns, nb, 2 * SLAB_G, NQ, LANES), F32))
        out_specs.append(pl.BlockSpec((None, None, 2 * SLAB_G, NQ, LANES), lambda k, b: (k, b, 0, 0, 0)))
    scratch = [
        pltpu.VMEM((NJ * PITCH, LANES), F32),
        pltpu.VMEM((SLAB_G, CHUNK, S5_CH, NJ), BF16),
        pltpu.VMEM((SLAB_G, 2, NQ * PITCH, LANES), F32),
        pltpu.VMEM((SLAB_G, 2, NQ * PITCH, LANES), F32),
        pltpu.VMEM((SLAB_G, CK, NJ), F32),
        pltpu.VMEM((NJ * PITCH, LANES), F32),
        pltpu.VMEM((2 * SLAB_G, NQ, LANES), F32),
        pltpu.VMEM((2 * SLAB_G, NQ, LANES), F32),
        pltpu.VMEM((2 * SLAB_G, NQ, LANES), F32),
    ]
    res = pl.pallas_call(
        functools.partial(_s5_core_kernel, chain=chain, has_pos=has_pos, want_z=want_z),
        out_shape=out_shape,
        grid=(ns, nb),
        in_specs=in_specs,
        out_specs=out_specs,
        scratch_shapes=scratch,
        compiler_params=pltpu.CompilerParams(vmem_limit_bytes=VMEM_LIMIT),
        name="s5_core_lat" if chain else ("s5_core_ctx" if want_z else "s5_state_ctx"),
    )(*args)
    if chain:
        return res[0]
    return (res[0], res[1]) if want_z else (None, res[0])


def _s5_prep_kernel(lr_ref, li_ref, ldt_ref, bri_ref, bir_ref, cri_ref, cir_ref,
                    mi_ref, ws_ref, wo_ref, tab_ref, t1_ref, t2_ref, u1_ref, u2_ref, x_ref, cf_ref):
    nrow = 2 * SLAB_G
    half = LANES // 2
    lane = lax.broadcasted_iota(jnp.int32, (nrow, LANES), 1)
    first = lane < half
    fwd_row = lax.broadcasted_iota(jnp.int32, (nrow, LANES), 0) % 2 == 0

    lr = jnp.minimum(lr_ref[...], LAMBDA_RE_MAX)
    li = li_ref[...]
    dt = jnp.exp(ldt_ref[...])
    mag = jnp.exp(lr * dt)
    ar = mag * jnp.cos(li * dt)
    ai = mag * jnp.sin(li * dt)
    den = lr * lr + li * li
    nr = ar - 1.0
    coef_re = (nr * lr + ai * li) / den
    coef_im = (ai * lr - nr * li) / den
    cf_ref[0] = coef_re
    cf_ref[1] = jnp.where(first, -coef_im, coef_im)

    def powers(a_re, a_im):
        a1 = a_re
        a2 = jnp.where(first, -a_im, a_im)
        p = jnp.where(first, 1.0, 0.0).astype(F32)
        w = jnp.where(first, 0.0, 1.0).astype(F32)
        out = [(p, w)]
        for _ in range(CHUNK):
            p, w = a1 * p + a2 * w, a1 * w - a2 * p
            out.append((p, w))
        return out

    pw = powers(ar, ai)
    for k, (p, w) in enumerate(pw):
        t1_ref[k] = jnp.where(first, p, w)
        t2_ref[k] = jnp.where(first, -w, p)
        u1_ref[k] = jnp.where(first, p, -w)
        u2_ref[k] = jnp.where(first, -w, -p)
    p16, w16 = pw[CHUNK]
    a16_re = jnp.where(first, p16, w16)
    a16_im = jnp.where(first, w16, p16)
    qw = powers(a16_re, a16_im)
    q1 = [jnp.where(first, p, w) for p, w in qw]
    q2 = [jnp.where(first, -w, p) for p, w in qw]
    tab_ref[0] = q1[1]
    tab_ref[1] = q2[1]
    tab_ref[2] = q1[CHUNK]
    tab_ref[3] = q2[CHUNK]
    for r in range(4, 8):
        tab_ref[r] = jnp.zeros((nrow, LANES), F32)
    for jp in range(CHUNK):
        tab_ref[8 + jp] = jnp.where(fwd_row, q1[jp], q1[CHUNK - 1 - jp])
        tab_ref[8 + CHUNK + jp] = jnp.where(fwd_row, q2[jp], q2[CHUNK - 1 - jp])

    lane2 = lax.broadcasted_iota(jnp.int32, (S5_CH, CK), 1)

    def group_body(g, carry):
        mt = [None] * CHUNK
        for d in range(2):
            r = g * 2 + d
            row = lambda v: jnp.broadcast_to(v, (S5_CH, LANES))
            bri = bri_ref[g, d]
            bir = bir_ref[g, d]
            cc1 = row(cf_ref[0, pl.ds(r, 1), :])
            cc2 = row(cf_ref[1, pl.ds(r, 1), :])
            bb_ri = cc1 * bri + cc2 * bir
            bb_ir = cc1 * bir - cc2 * bri
            cri = cri_ref[g, d]
            cir = cir_ref[g, d]
            for s in range(CHUNK):
                k = CHUNK - 1 - s if d == 0 else s
                xk = row(t1_ref[k, pl.ds(r, 1), :]) * bb_ri + row(t2_ref[k, pl.ds(r, 1), :]) * bb_ir
                x_ref[s * S5_CH:(s + 1) * S5_CH, :] = xk
                ws_ref[g, s * S5_CH:(s + 1) * S5_CH, d * LANES:(d + 1) * LANES] = xk.astype(BF16)
            for t in range(CHUNK):
                k = t + 1 if d == 0 else CHUNK - t
                wk = row(u1_ref[k, pl.ds(r, 1), :]) * cri + row(u2_ref[k, pl.ds(r, 1), :]) * cir
                wo_ref[g, t * S5_CH:(t + 1) * S5_CH, d * LANES:(d + 1) * LANES] = wk.astype(BF16)
            cneg = jnp.where(first[:S5_CH], cri, -cri)
            krow = lax.dot_general(cneg, x_ref[...], (((1,), (1,)), ((), ())),
                                   preferred_element_type=F32, precision=HI)
            for t in range(CHUNK):
                if d == 0:
                    sh = (CHUNK - 1 - t) * S5_CH
                    rolled = pltpu.roll(krow, (CK - sh) % CK, 1) if sh else krow
                    mt[t] = jnp.where(lane2 < (t + 1) * S5_CH, rolled, 0.0)
                else:
                    sh = t * S5_CH
                    rolled = pltpu.roll(krow, sh, 1) if sh else krow
                    mt[t] = mt[t] + jnp.where(lane2 >= sh, rolled, 0.0)
        for t in range(CHUNK):
            mi_ref[g, t * S5_CH:(t + 1) * S5_CH, :] = mt[t].astype(BF16)
        return carry

    lax.fori_loop(0, SLAB_G, group_body, 0)


def _s5_matrices(lam_re, lam_im, log_dt, b_re, b_im, c_re, c_im):
    _, g, p = lam_re.shape
    assert 2 * p == LANES and g % SLAB_G == 0
    ns = g // SLAB_G
    nrow = 2 * SLAB_G

    def rows(v):
        v = jnp.transpose(v.astype(F32), (1, 0, 2)).reshape(ns, nrow, p)
        return jnp.concatenate([v, v], axis=-1)

    def pairs(vr, vi):
        return jnp.transpose(jnp.concatenate([vr.astype(F32), vi.astype(F32)], axis=-1), (1, 0, 2, 3))

    ldt = jnp.broadcast_to(log_dt.astype(F32)[..., None], (2, g, p))
    b_re_t = jnp.swapaxes(b_re, -1, -2)
    b_im_t = jnp.swapaxes(b_im, -1, -2)
    blk4 = pl.BlockSpec((SLAB_G, 2, S5_CH, LANES), lambda k: (k, 0, 0, 0))
    blk3 = pl.BlockSpec((None, nrow, LANES), lambda k: (k, 0, 0))
    mat = jax.ShapeDtypeStruct((g, CK, CK), BF16)
    mat_spec = pl.BlockSpec((SLAB_G, CK, CK), lambda k: (k, 0, 0))
    return pl.pallas_call(
        _s5_prep_kernel,
        out_shape=[mat, mat, mat, jax.ShapeDtypeStruct((ns, TAB_ROWS, nrow, LANES), F32)],
        grid=(ns,),
        in_specs=[blk3, blk3, blk3, blk4, blk4, blk4, blk4],
        out_specs=[mat_spec, mat_spec, mat_spec,
                   pl.BlockSpec((None, TAB_ROWS, nrow, LANES), lambda k: (k, 0, 0, 0))],
        scratch_shapes=[pltpu.VMEM((CHUNK + 1, nrow, LANES), F32)] * 4
        + [pltpu.VMEM((CK, LANES), F32), pltpu.VMEM((2, nrow, LANES), F32)],
        compiler_params=pltpu.CompilerParams(vmem_limit_bytes=VMEM_LIMIT),
        name="s5_prep",
    )(rows(lam_re), rows(lam_im), rows(ldt), pairs(b_re_t, b_im_t), pairs(b_im_t, b_re_t),
      pairs(c_re, c_im), pairs(c_im, c_re))


def _tail_body(x, m, mod, lnp, wm_ref, bm_ref, w1_ref, w2_ref, *, glu, alpha, hidden_chunk):
    tm, d = x.shape
    zz = jnp.dot(m, wm_ref[...], preferred_element_type=F32) + bm_ref[...]
    if glu:
        mix = zz[:, :d] * jax.nn.sigmoid(zz[:, d:])
    else:
        mix = zz
    x1 = _ln(alpha * x + mod[2:3] * mix, lnp[0:1], lnp[2:3])
    h = (x1 * (1.0 + mod[4:5]) + mod[3:4]).astype(BF16)
    dff = w1_ref.shape[1]
    acc = jnp.zeros((tm, d), F32)
    for c in range(dff // hidden_chunk):
        lo = c * hidden_chunk
        hid = jnp.dot(h, w1_ref[:, lo:lo + hidden_chunk], preferred_element_type=F32)
        hid = jnp.square(jnp.maximum(hid, 0.0)).astype(BF16)
        acc = acc + jnp.dot(hid, w2_ref[lo:lo + hidden_chunk, :], preferred_element_type=F32)
    return _ln(alpha * x1 + mod[5:6] * acc, lnp[1:2], lnp[3:4])


def _conv_front_kernel(x_ref, mod_ref, w1_ref, b1_ref, wdw_ref, cvp_ref, o_ref, a_ref, cv_ref,
                       *, width, pw_tile, tile):
    length, d = x_ref.shape
    ns = d // LANES
    half = width // 2
    pad = HALO
    a_ref[:, 0:pad, :] = jnp.zeros((ns, pad, LANES), F32)
    a_ref[:, pad + length:, :] = jnp.zeros((ns, pad, LANES), F32)
    mod = mod_ref[...]
    shift = mod[0:1]
    scale1 = 1.0 + mod[1:2]

    def pw_body(i, carry):
        r0 = pl.multiple_of(i * pw_tile, pw_tile)
        h = (x_ref[pl.ds(r0, pw_tile), :] * scale1 + shift).astype(BF16)
        a = jnp.dot(h, w1_ref[...], preferred_element_type=F32) + b1_ref[...]
        a = a[:, :d] * jax.nn.sigmoid(a[:, d:])
        for sl in range(ns):
            a_ref[sl, pl.ds(pad + r0, pw_tile), :] = a[:, sl * LANES:(sl + 1) * LANES]
        return carry

    lax.fori_loop(0, length // pw_tile, pw_body, 0)

    def dw_body(sl, carry):
        for r0 in range(0, length, tile):
            acc = jnp.zeros((tile, LANES), F32)
            for k in range(width):
                lo = r0 + pad - half + k
                acc = acc + wdw_ref[sl, k:k + 1, :] * a_ref[sl, lo:lo + tile, :]
            cv_ref[sl, r0:r0 + tile, :] = acc
        return carry

    lax.fori_loop(0, ns, dw_body, 0)

    cvp = cvp_ref[...]

    def ln_body(i, carry):
        r0 = pl.multiple_of(i * tile, tile)
        cv = jnp.concatenate([cv_ref[sl, pl.ds(r0, tile), :] for sl in range(ns)], axis=1)
        y = _ln(cv + cvp[0:1], cvp[1:2], cvp[2:3])
        o_ref[pl.ds(r0, tile), :] = (y * jax.nn.sigmoid(y)).astype(BF16)
        return carry

    lax.fori_loop(0, length // tile, ln_body, 0)


def _conv_front(xs, mod, w1_all, j, b1, wdw, cvp, *, width):
    nb, length, d = xs.shape
    ns = d // LANES
    tile = 128
    pw_tile = 256
    assert length % pw_tile == 0 and width // 2 < HALO
    per_b = mod.shape[0] == nb and nb > 1
    mod_map = (lambda b: (b, 0, 0)) if per_b else (lambda b: (0, 0, 0))
    return pl.pallas_call(
        functools.partial(_conv_front_kernel, width=width, pw_tile=pw_tile, tile=tile),
        out_shape=jax.ShapeDtypeStruct((nb, length, d), BF16),
        grid=(nb,),
        in_specs=[
            pl.BlockSpec((None, length, d), lambda b: (b, 0, 0)),
            pl.BlockSpec((None, 8, d), mod_map),
            _layer_weight(w1_all, j, 1),
            pl.BlockSpec((1, 2 * d), lambda b: (0, 0)),
            pl.BlockSpec(wdw.shape, lambda b: (0, 0, 0)),
            pl.BlockSpec((8, d), lambda b: (0, 0)),
        ],
        out_specs=pl.BlockSpec((None, length, d), lambda b: (b, 0, 0)),
        scratch_shapes=[pltpu.VMEM((ns, length + 2 * HALO, LANES), F32), pltpu.VMEM((ns, length, LANES), F32)],
        compiler_params=pltpu.CompilerParams(vmem_limit_bytes=VMEM_LIMIT),
        name="conv_front",
    )(xs, mod, w1_all, b1, wdw, cvp)


def _tail_mlp_kernel(*refs, glu, has_pos, alpha, hidden_chunk):
    it = iter(refs)
    x_ref = next(it)
    pos_ref = next(it) if has_pos else None
    m_ref, mod_ref, wm_ref, bm_ref, ln_ref, w1_ref, w2_ref, o_ref = it
    x = x_ref[...]
    if has_pos:
        x = x + pos_ref[...]
    o_ref[...] = _tail_body(x, m_ref[...], mod_ref[...], ln_ref[...], wm_ref, bm_ref, w1_ref, w2_ref,
                            glu=glu, alpha=alpha, hidden_chunk=hidden_chunk)


def _layer_weight(stack, layer, grid_rank):
    zeros = (0,) * (stack.ndim - 1)
    if grid_rank == 1:
        index_map = lambda b: (layer,) + zeros
    else:
        index_map = lambda b, t: (layer,) + zeros
    return pl.BlockSpec((None,) + stack.shape[1:], index_map, pipeline_mode=pl.Buffered(1))


def _tail_mlp(xs, pos, mix_in, mod, wm_all, jm, bm, lnp, w1_all, w2_all, layer, *, glu, alpha):
    nb, length, d = xs.shape
    tm = min(512, length)
    assert length % tm == 0
    has_pos = pos is not None
    per_b = mod.shape[0] == nb and nb > 1
    mod_map = (lambda b, t: (b, 0, 0)) if per_b else (lambda b, t: (0, 0, 0))
    const = lambda b, t: (0, 0)
    in_specs = [pl.BlockSpec((None, tm, d), lambda b, t: (b, t, 0))]
    args = [xs]
    if has_pos:
        in_specs.append(pl.BlockSpec((tm, d), lambda b, t: (t, 0)))
        args.append(pos)
    in_specs += [
        pl.BlockSpec((None, tm, d), lambda b, t: (b, t, 0)),
        pl.BlockSpec((None, 8, d), mod_map),
        _layer_weight(wm_all, jm, 2),
        pl.BlockSpec(bm.shape, const),
        pl.BlockSpec((8, d), const),
        _layer_weight(w1_all, layer, 2),
        _layer_weight(w2_all, layer, 2),
    ]
    args += [mix_in, mod, wm_all, bm, lnp, w1_all, w2_all]
    return pl.pallas_call(
        functools.partial(_tail_mlp_kernel, glu=glu, has_pos=has_pos, alpha=alpha,
                          hidden_chunk=min(1024, w1_all.shape[2])),
        out_shape=jax.ShapeDtypeStruct((nb, length, d), F32),
        grid=(nb, length // tm),
        in_specs=in_specs,
        out_specs=pl.BlockSpec((None, tm, d), lambda b, t: (b, t, 0)),
        compiler_params=pltpu.CompilerParams(vmem_limit_bytes=VMEM_LIMIT),
        name="tail_glu" if glu else "tail_pw",
    )(*args)


def _sincos_1d(pos, dim):
    quarter = dim // 2
    omega = POS_TEMP ** (-jnp.arange(quarter, dtype=F32) / quarter)
    ang = pos[:, None] * omega[None, :]
    return jnp.concatenate([jnp.sin(ang), jnp.cos(ang)], axis=-1)


def _grid_pos_embed(rows, dim):
    row_idx = jnp.repeat(jnp.arange(rows), GRID_W).astype(F32)
    col_idx = jnp.tile(jnp.arange(GRID_W), rows).astype(F32)
    return jnp.concatenate([_sincos_1d(row_idx, dim // 2), _sincos_1d(col_idx, dim // 2)], axis=-1)


def _rows8(*rows):
    d = rows[0].shape[-1]
    out = jnp.stack([r.astype(F32).reshape(d) for r in rows])
    return jnp.concatenate([out, jnp.zeros((8 - len(rows), d), F32)], axis=0)


def kernel(x, c, ctx, c_ctx, w_ada, b_ada, ln_gain, ln_bias, s5_lam_re, s5_lam_im, s5_log_dt, s5_b_re, s5_b_im, s5_c_re, s5_c_im, s5_d, s5_w_glu, s5_b_glu, cv_w_pw1, cv_b_pw1, cv_w_dw, cv_b_dw, cv_ln_g, cv_ln_b, cv_w_pw2, cv_b_pw2, mlp_w1, mlp_w2):
    bsz, length, d = x.shape
    lctx = ctx.shape[1]
    depth = w_ada.shape[0]
    width = cv_w_dw.shape[1]
    alpha = (2.0 * depth) ** 0.25
    mixers = ("s5", "conv")
    kinds = [mixers[i % 2] for i in range(depth)]
    assert length == ROWS and NQ * lctx == ROWS and bsz % NQ == 0 and d % LANES == 0
    nbc = bsz // NQ

    pos = _grid_pos_embed(length // GRID_W, d).astype(x.dtype)
    nrow = -(-(bsz + 1) // 8) * 8
    cc = jnp.concatenate([c.astype(F32), c_ctx.astype(F32)[None], jnp.zeros((nrow - bsz - 1, d), F32)], axis=0)
    mods = _ada_mods(cc, w_ada.astype(F32), b_ada.astype(F32))
    mods = jnp.pad(mods.reshape(depth, nrow, 6, d), ((0, 0), (0, 0), (0, 2), (0, 0)))

    w1_all = mlp_w1.astype(BF16)
    w2_all = mlp_w2.astype(BF16)
    glu_all = s5_w_glu.astype(BF16)
    pw1_all = cv_w_pw1.astype(BF16)
    pw2_all = cv_w_pw2.astype(BF16)

    s5_j = 0
    cv_j = 0
    for i, kind in enumerate(kinds):
        ctx_needed_later = any(k == "s5" for k in kinds[i + 1:])
        mod_lat = mods[i, :bsz]
        mod_ctx = mods[i, bsz:bsz + 1]
        lnp = _rows8(ln_gain[i, 0], ln_gain[i, 1], ln_bias[i, 0], ln_bias[i, 1])
        pos_i = pos if i == 0 else None
        if kind == "s5":
            j = s5_j
            s5_j += 1
            mats = _s5_matrices(s5_lam_re[j], s5_lam_im[j], s5_log_dt[j], s5_b_re[j], s5_b_im[j],
                                s5_c_re[j], s5_c_im[j])
            dskip = s5_d[j].astype(F32).reshape(1, d)
            bm = s5_b_glu[j].astype(F32).reshape(1, 2 * d)
            zc, hfin = _s5_core(ctx.reshape(nbc, ROWS, d), None, mod_ctx, dskip, mats, None, chain=False,
                                want_z=ctx_needed_later)
            h0 = jnp.transpose(hfin, (0, 1, 3, 2, 4)).reshape(d // LANES, bsz, 2 * SLAB_G, LANES)
            z = _s5_core(x, pos_i, mod_lat, dskip, mats, h0, chain=True)
            x_new = _tail_mlp(x, pos_i, z, mod_lat, glu_all, j, bm, lnp, w1_all, w2_all, i, glu=True, alpha=alpha)
            if ctx_needed_later:
                ctx = _tail_mlp(ctx, None, zc.reshape(bsz, lctx, d), mod_ctx, glu_all, j, bm, lnp, w1_all, w2_all, i,
                                glu=True, alpha=alpha)
            x = x_new
        else:
            j = cv_j
            cv_j += 1
            bp1 = cv_b_pw1[j].astype(F32).reshape(1, 2 * d)
            assert pos_i is None
            wdw = jnp.concatenate([cv_w_dw[j].astype(F32), jnp.zeros((-width % 8, d), F32)], axis=0)
            wdw = jnp.transpose(wdw.reshape(-1, d // LANES, LANES), (1, 0, 2))
            cvp = _rows8(cv_b_dw[j], cv_ln_g[j], cv_ln_b[j])
            bm = cv_b_pw2[j].astype(F32).reshape(1, d)
            a = _conv_front(x, mod_lat, pw1_all, j, bp1, wdw, cvp, width=width)
            x_new = _tail_mlp(x, None, a, mod_lat, pw2_all, j, bm, lnp, w1_all, w2_all, i, glu=False, alpha=alpha)
            if ctx_needed_later:
                ac = _conv_front(ctx, mod_ctx, pw1_all, j, bp1, wdw, cvp, width=width)
                ctx = _tail_mlp(ctx, None, ac, mod_ctx, pw2_all, j, bm, lnp, w1_all, w2_all, i, glu=False, alpha=alpha)
            x = x_new
    return x
```

```python
import functools
import math

import jax
import jax.numpy as jnp
from jax import lax
from jax.experimental import pallas as pl
from jax.experimental.pallas import tpu as pltpu

F32 = jnp.float32
BF16 = jnp.bfloat16

LANES = 128
S5_CH = 16
CHUNK = 16
CK = CHUNK * S5_CH
SLAB_G = LANES // S5_CH
NQ = 8
NJ = NQ * CHUNK
ROWS = NJ * CHUNK
PITCH = 24
TAB_ROWS = 40
HALO = 16
GRID_W = 64
POS_TEMP = 10000.0
LN_EPS = 1e-5
LAMBDA_RE_MAX = -1e-4
VMEM_LIMIT = 56 * 1024 * 1024
HI = lax.Precision.HIGHEST


def _ln(v, g, b):
    mu = jnp.mean(v, axis=-1, keepdims=True)
    d = v - mu
    var = jnp.mean(d * d, axis=-1, keepdims=True)
    return d * lax.rsqrt(var + LN_EPS) * g + b


def _ada_kernel(cc_ref, w_ref, b_ref, o_ref):
    cc = cc_ref[...]
    s = cc * jax.nn.sigmoid(cc)
    o_ref[...] = jnp.dot(s, w_ref[...], preferred_element_type=F32, precision=HI) + b_ref[...]


def _ada_mods(cc, w_ada, b_ada):
    depth, d, d6 = w_ada.shape
    rows = cc.shape[0]
    tn = 1536 if d6 % 1536 == 0 else d6
    return pl.pallas_call(
        _ada_kernel,
        out_shape=jax.ShapeDtypeStruct((depth, rows, d6), F32),
        grid=(depth, d6 // tn),
        in_specs=[
            pl.BlockSpec((rows, d), lambda i, j: (0, 0)),
            pl.BlockSpec((None, d, tn), lambda i, j: (i, 0, j)),
            pl.BlockSpec((None, 1, tn), lambda i, j: (i, 0, j)),
        ],
        out_specs=pl.BlockSpec((None, rows, tn), lambda i, j: (i, 0, j)),
        compiler_params=pltpu.CompilerParams(vmem_limit_bytes=VMEM_LIMIT),
        name="ada_mods",
    )(cc, w_ada, b_ada.reshape(depth, 1, d6))


def _s5_core_kernel(*refs, chain, has_pos, want_z):
    it = iter(refs)
    x_ref = next(it)
    pos_ref = next(it) if has_pos else None
    mod_ref = next(it)
    dsk_ref = next(it)
    mi_ref = next(it)
    ws_ref = next(it)
    wo_ref = next(it)
    tab_ref = next(it)
    h0_ref = next(it) if chain else None
    z_ref = next(it) if want_z else None
    hfin_ref = None if chain else next(it)
    u_ref, r_ref, s_ref, e_ref, yt_ref, y_ref, f_ref, fw_ref, e16_ref = it
    half = LANES // 2

    mod = mod_ref[...]
    scale1 = 1.0 + mod[1:2]
    shift1 = mod[0:1]
    for j in range(NJ):
        x = x_ref[j * CHUNK:(j + 1) * CHUNK, :]
        if has_pos:
            x = x + pos_ref[j * CHUNK:(j + 1) * CHUNK, :]
        u_ref[j * PITCH:j * PITCH + CHUNK, :] = x * scale1 + shift1

    for s in range(CHUNK):
        xs = u_ref[pl.ds(s, NJ, stride=PITCH), :]
        r_ref[:, s] = xs.T.reshape(SLAB_G, S5_CH, NJ).astype(BF16)

    for g in range(SLAB_G):
        rg = r_ref[g].reshape(CK, NJ)
        if want_z:
            yt_ref[g] = jnp.dot(mi_ref[g], rg, preferred_element_type=F32)
        st = lax.dot_general(rg, ws_ref[g], (((0,), (0,)), ((), ())),
                             preferred_element_type=F32)
        for q in range(NQ):
            s_ref[g, 0, q * PITCH:q * PITCH + CHUNK, :] = st[q * CHUNK:(q + 1) * CHUNK, :LANES]
            s_ref[g, 1, q * PITCH:q * PITCH + CHUNK, :] = st[q * CHUNK:(q + 1) * CHUNK, LANES:]

    def cstep(a1, a2, e, w, s, sw):
        return a1 * e + a2 * w + s, a1 * w - a2 * e + sw

    def scan_dir(g, d):
        sg = s_ref.at[g, d]
        eg = e_ref.at[g, d]
        gd = g * 2 + d
        tab = lambda r: tab_ref[r, gd:gd + 1, :]
        a1 = tab(0)
        a2 = tab(1)
        e = jnp.zeros((NQ, LANES), F32)
        w = e
        order = range(CHUNK) if d == 0 else range(CHUNK - 1, -1, -1)
        eloc = []
        for jp in order:
            eloc.append((jp, e))
            s = sg[pl.ds(jp, NQ, stride=PITCH), :]
            e, w = cstep(a1, a2, e, w, s, pltpu.roll(s, half, 1))
        if not chain:
            if want_z:
                for jp, el in eloc:
                    eg[pl.ds(jp, NQ, stride=PITCH), :] = el
            hfin_ref[gd] = e
            return
        f_ref[gd] = e
        fw_ref[gd] = w
        b1 = tab(2)
        b2 = tab(3)
        cst = jnp.broadcast_to(h0_ref[gd:gd + 1, :], (NQ, LANES))
        cstw = pltpu.roll(cst, half, 1)
        qs = range(NQ) if d == 0 else range(NQ - 1, -1, -1)
        for q in qs:
            e16_ref[gd, q:q + 1, :] = cst[0:1, :]
            cst, cstw = cstep(b1, b2, cst, cstw, f_ref[gd, q:q + 1, :], fw_ref[gd, q:q + 1, :])
        e16 = e16_ref[gd]
        e16r = pltpu.roll(e16, half, 1)
        for jp, el in eloc:
            eg[pl.ds(jp, NQ, stride=PITCH), :] = el + tab(8 + jp) * e16 + tab(8 + CHUNK + jp) * e16r

    for g in range(SLAB_G):
        scan_dir(g, 0)
        scan_dir(g, 1)

    if not want_z:
        return

    for g in range(SLAB_G):
        ef = jnp.concatenate([e_ref[g, 0, q * PITCH:q * PITCH + CHUNK, :] for q in range(NQ)], axis=0)
        eb = jnp.concatenate([e_ref[g, 1, q * PITCH:q * PITCH + CHUNK, :] for q in range(NQ)], axis=0)
        et = jnp.concatenate([ef, eb], axis=1).astype(BF16)
        yo = lax.dot_general(wo_ref[g], et, (((1,), (1,)), ((), ())),
                             preferred_element_type=F32)
        yt_ref[g] = yt_ref[g] + yo

    for t in range(CHUNK):
        zt = yt_ref[:, t * S5_CH:(t + 1) * S5_CH, :].reshape(LANES, NJ)
        y_ref[pl.ds(t, NJ, stride=PITCH), :] = zt.T

    dsk = dsk_ref[...]
    for j in range(NJ):
        y = y_ref[j * PITCH:j * PITCH + CHUNK, :] + dsk * u_ref[j * PITCH:j * PITCH + CHUNK, :]
        z_ref[j * CHUNK:(j + 1) * CHUNK, :] = (0.5 * y * (1.0 + lax.erf(y * (1.0 / math.sqrt(2.0))))).astype(BF16)


def _s5_core(xs, pos, mod, dskip, mats, h0, *, chain, want_z=True):
    nb, rows, d = xs.shape
    assert rows == ROWS and d % LANES == 0 and (want_z or not chain)
    ns = d // LANES
    mi, ws, wo, tab = mats
    has_pos = pos is not None
    per_b = mod.shape[0] == nb and nb > 1
    in_specs = [pl.BlockSpec((None, ROWS, LANES), lambda k, b: (b, 0, k))]
    args = [xs]
    if has_pos:
        in_specs.append(pl.BlockSpec((ROWS, LANES), lambda k, b: (0, k)))
        args.append(pos)
    if per_b:
        in_specs.append(pl.BlockSpec((None, 8, LANES), lambda k, b: (b, 0, k)))
    else:
        in_specs.append(pl.BlockSpec((None, 8, LANES), lambda k, b: (0, 0, k)))
    args.append(mod)
    in_specs += [
        pl.BlockSpec((1, LANES), lambda k, b: (0, k)),
        pl.BlockSpec((SLAB_G, CK, CK), lambda k, b: (k, 0, 0)),
        pl.BlockSpec((SLAB_G, CK, 256), lambda k, b: (k, 0, 0)),
        pl.BlockSpec((SLAB_G, CK, 256), lambda k, b: (k, 0, 0)),
        pl.BlockSpec((None, TAB_ROWS, 2 * SLAB_G, LANES), lambda k, b: (k, 0, 0, 0)),
    ]
    args += [dskip, mi, ws, wo, tab]
    out_shape = []
    out_specs = []
    if want_z:
        out_shape.append(jax.ShapeDtypeStruct((nb, ROWS, d), BF16))
        out_specs.append(pl.BlockSpec((None, ROWS, LANES), lambda k, b: (b, 0, k)))
    if chain:
        in_specs.append(pl.BlockSpec((None, None, 2 * SLAB_G, LANES), lambda k, b: (k, b, 0, 0)))
        args.append(h0)
    else:
        out_shape.append(jax.ShapeDtypeStruct((ns, nb, 2 * SLAB_G, NQ, LANES), F32))
        out_specs.append(pl.BlockSpec((None, None, 2 * SLAB_G, NQ, LANES), lambda k, b: (k, b, 0, 0, 0)))
    scratch = [
        pltpu.VMEM((NJ * PITCH, LANES), F32),
        pltpu.VMEM((SLAB_G, CHUNK, S5_CH, NJ), BF16),
        pltpu.VMEM((SLAB_G, 2, NQ * PITCH, LANES), F32),
        pltpu.VMEM((SLAB_G, 2, NQ * PITCH, LANES), F32),
        pltpu.VMEM((SLAB_G, CK, NJ), F32),
        pltpu.VMEM((NJ * PITCH, LANES), F32),
        pltpu.VMEM((2 * SLAB_G, NQ, LANES), F32),
        pltpu.VMEM((2 * SLAB_G, NQ, LANES), F32),
        pltpu.VMEM((2 * SLAB_G, NQ, LANES), F32),
    ]
    res = pl.pallas_call(
        functools.partial(_s5_core_kernel, chain=chain, has_pos=has_pos, want_z=want_z),
        out_shape=out_shape,
        grid=(ns, nb),
        in_specs=in_specs,
        out_specs=out_specs,
        scratch_shapes=scratch,
        compiler_params=pltpu.CompilerParams(vmem_limit_bytes=VMEM_LIMIT),
        name="s5_core_lat" if chain else ("s5_core_ctx" if want_z else "s5_state_ctx"),
    )(*args)
    if chain:
        return res[0]
    return (res[0], res[1]) if want_z else (None, res[0])


def _s5_prep_kernel(lr_ref, li_ref, ldt_ref, bri_ref, bir_ref, cri_ref, cir_ref,
                    mi_ref, ws_ref, wo_ref, tab_ref, t1_ref, t2_ref, u1_ref, u2_ref, x_ref, cf_ref):
    nrow = 2 * SLAB_G
    half = LANES // 2
    lane = lax.broadcasted_iota(jnp.int32, (nrow, LANES), 1)
    first = lane < half
    fwd_row = lax.broadcasted_iota(jnp.int32, (nrow, LANES), 0) % 2 == 0

    lr = jnp.minimum(lr_ref[...], LAMBDA_RE_MAX)
    li = li_ref[...]
    dt = jnp.exp(ldt_ref[...])
    mag = jnp.exp(lr * dt)
    ar = mag * jnp.cos(li * dt)
    ai = mag * jnp.sin(li * dt)
    den = lr * lr + li * li
    nr = ar - 1.0
    coef_re = (nr * lr + ai * li) / den
    coef_im = (ai * lr - nr * li) / den
    cf_ref[0] = coef_re
    cf_ref[1] = jnp.where(first, -coef_im, coef_im)

    def powers(a_re, a_im):
        a1 = a_re
        a2 = jnp.where(first, -a_im, a_im)
        p = jnp.where(first, 1.0, 0.0).astype(F32)
        w = jnp.where(first, 0.0, 1.0).astype(F32)
        out = [(p, w)]
        for _ in range(CHUNK):
            p, w = a1 * p + a2 * w, a1 * w - a2 * p
            out.append((p, w))
        return out

    pw = powers(ar, ai)
    for k, (p, w) in enumerate(pw):
        t1_ref[k] = jnp.where(first, p, w)
        t2_ref[k] = jnp.where(first, -w, p)
        u1_ref[k] = jnp.where(first, p, -w)
        u2_ref[k] = jnp.where(first, -w, -p)
    p16, w16 = pw[CHUNK]
    a16_re = jnp.where(first, p16, w16)
    a16_im = jnp.where(first, w16, p16)
    qw = powers(a16_re, a16_im)
    q1 = [jnp.where(first, p, w) for p, w in qw]
    q2 = [jnp.where(first, -w, p) for p, w in qw]
    tab_ref[0] = q1[1]
    tab_ref[1] = q2[1]
    tab_ref[2] = q1[CHUNK]
    tab_ref[3] = q2[CHUNK]
    for r in range(4, 8):
        tab_ref[r] = jnp.zeros((nrow, LANES), F32)
    for jp in range(CHUNK):
        tab_ref[8 + jp] = jnp.where(fwd_row, q1[jp], q1[CHUNK - 1 - jp])
        tab_ref[8 + CHUNK + jp] = jnp.where(fwd_row, q2[jp], q2[CHUNK - 1 - jp])

    lane2 = lax.broadcasted_iota(jnp.int32, (S5_CH, CK), 1)

    def group_body(g, carry):
        mt = [None] * CHUNK
        for d in range(2):
            r = g * 2 + d
            row = lambda v: jnp.broadcast_to(v, (S5_CH, LANES))
            bri = bri_ref[g, d]
            bir = bir_ref[g, d]
            cc1 = row(cf_ref[0, pl.ds(r, 1), :])
            cc2 = row(cf_ref[1, pl.ds(r, 1), :])
            bb_ri = cc1 * bri + cc2 * bir
            bb_ir = cc1 * bir - cc2 * bri
            cri = cri_ref[g, d]
            cir = cir_ref[g, d]
            for s in range(CHUNK):
                k = CHUNK - 1 - s if d == 0 else s
                xk = row(t1_ref[k, pl.ds(r, 1), :]) * bb_ri + row(t2_ref[k, pl.ds(r, 1), :]) * bb_ir
                x_ref[s * S5_CH:(s + 1) * S5_CH, :] = xk
                ws_ref[g, s * S5_CH:(s + 1) * S5_CH, d * LANES:(d + 1) * LANES] = xk.astype(BF16)
            for t in range(CHUNK):
                k = t + 1 if d == 0 else CHUNK - t
                wk = row(u1_ref[k, pl.ds(r, 1), :]) * cri + row(u2_ref[k, pl.ds(r, 1), :]) * cir
                wo_ref[g, t * S5_CH:(t + 1) * S5_CH, d * LANES:(d + 1) * LANES] = wk.astype(BF16)
            cneg = jnp.where(first[:S5_CH], cri, -cri)
            krow = lax.dot_general(cneg, x_ref[...], (((1,), (1,)), ((), ())),
                                   preferred_element_type=F32, precision=HI)
            for t in range(CHUNK):
                if d == 0:
                    sh = (CHUNK - 1 - t) * S5_CH
                    rolled = pltpu.roll(krow, (CK - sh) % CK, 1) if sh else krow
                    mt[t] = jnp.where(lane2 < (t + 1) * S5_CH, rolled, 0.0)
                else:
                    sh = t * S5_CH
                    rolled = pltpu.roll(krow, sh, 1) if sh else krow
                    mt[t] = mt[t] + jnp.where(lane2 >= sh, rolled, 0.0)
        for t in range(CHUNK):
            mi_ref[g, t * S5_CH:(t + 1) * S5_CH, :] = mt[t].astype(BF16)
        return carry

    lax.fori_loop(0, SLAB_G, group_body, 0, unroll=True)


def _s5_matrices(lam_re, lam_im, log_dt, b_re, b_im, c_re, c_im):
    _, g, p = lam_re.shape
    assert 2 * p == LANES and g % SLAB_G == 0
    ns = g // SLAB_G
    nrow = 2 * SLAB_G

    def rows(v):
        v = jnp.transpose(v.astype(F32), (1, 0, 2)).reshape(ns, nrow, p)
        return jnp.concatenate([v, v], axis=-1)

    def pairs(vr, vi):
        return jnp.transpose(jnp.concatenate([vr.astype(F32), vi.astype(F32)], axis=-1), (1, 0, 2, 3))

    ldt = jnp.broadcast_to(log_dt.astype(F32)[..., None], (2, g, p))
    b_re_t = jnp.swapaxes(b_re, -1, -2)
    b_im_t = jnp.swapaxes(b_im, -1, -2)
    blk4 = pl.BlockSpec((SLAB_G, 2, S5_CH, LANES), lambda k: (k, 0, 0, 0))
    blk3 = pl.BlockSpec((None, nrow, LANES), lambda k: (k, 0, 0))
    mat = jax.ShapeDtypeStruct((g, CK, CK), BF16)
    mat_spec = pl.BlockSpec((SLAB_G, CK, CK), lambda k: (k, 0, 0))
    return pl.pallas_call(
        _s5_prep_kernel,
        out_shape=[mat, mat, mat, jax.ShapeDtypeStruct((ns, TAB_ROWS, nrow, LANES), F32)],
        grid=(ns,),
        in_specs=[blk3, blk3, blk3, blk4, blk4, blk4, blk4],
        out_specs=[mat_spec, mat_spec, mat_spec,
                   pl.BlockSpec((None, TAB_ROWS, nrow, LANES), lambda k: (k, 0, 0, 0))],
        scratch_shapes=[pltpu.VMEM((CHUNK + 1, nrow, LANES), F32)] * 4
        + [pltpu.VMEM((CK, LANES), F32), pltpu.VMEM((2, nrow, LANES), F32)],
        compiler_params=pltpu.CompilerParams(vmem_limit_bytes=VMEM_LIMIT),
        name="s5_prep",
    )(rows(lam_re), rows(lam_im), rows(ldt), pairs(b_re_t, b_im_t), pairs(b_im_t, b_re_t),
      pairs(c_re, c_im), pairs(c_im, c_re))


def _tail_body(x, m, mod, lnp, wm_ref, bm_ref, w1_ref, w2_ref, *, glu, alpha, hidden_chunk):
    tm, d = x.shape
    zz = jnp.dot(m, wm_ref[...], preferred_element_type=F32) + bm_ref[...]
    if glu:
        mix = zz[:, :d] * jax.nn.sigmoid(zz[:, d:])
    else:
        mix = zz
    x1 = _ln(alpha * x + mod[2:3] * mix, lnp[0:1], lnp[2:3])
    h = (x1 * (1.0 + mod[4:5]) + mod[3:4]).astype(BF16)
    dff = w1_ref.shape[1]
    acc = jnp.zeros((tm, d), F32)
    for c in range(dff // hidden_chunk):
        lo = c * hidden_chunk
        hid = jnp.dot(h, w1_ref[:, lo:lo + hidden_chunk], preferred_element_type=F32)
        hid = jnp.square(jnp.maximum(hid, 0.0)).astype(BF16)
        acc = acc + jnp.dot(hid, w2_ref[lo:lo + hidden_chunk, :], preferred_element_type=F32)
    return _ln(alpha * x1 + mod[5:6] * acc, lnp[1:2], lnp[3:4])


def _conv_front_kernel(x_ref, mod_ref, w1_ref, b1_ref, wdw_ref, cvp_ref, o_ref, a_ref, cv_ref,
                       *, width, pw_tile, tile, ln_tile):
    length, d = x_ref.shape
    ns = d // LANES
    half = width // 2
    pad = HALO
    a_ref[:, 0:pad, :] = jnp.zeros((ns, pad, LANES), F32)
    a_ref[:, pad + length:, :] = jnp.zeros((ns, pad, LANES), F32)
    mod = mod_ref[...]
    shift = mod[0:1]
    scale1 = 1.0 + mod[1:2]

    def pw_body(i, carry):
        r0 = pl.multiple_of(i * pw_tile, pw_tile)
        h = (x_ref[pl.ds(r0, pw_tile), :] * scale1 + shift).astype(BF16)
        a = jnp.dot(h, w1_ref[...], preferred_element_type=F32) + b1_ref[...]
        a = a[:, :d] * jax.nn.sigmoid(a[:, d:])
        for sl in range(ns):
            a_ref[sl, pl.ds(pad + r0, pw_tile), :] = a[:, sl * LANES:(sl + 1) * LANES]
        return carry

    lax.fori_loop(0, length // pw_tile, pw_body, 0)

    def dw_body(sl, carry):
        for r0 in range(0, length, tile):
            acc = jnp.zeros((tile, LANES), F32)
            for k in range(width):
                lo = r0 + pad - half + k
                acc = acc + wdw_ref[sl, k:k + 1, :] * a_ref[sl, lo:lo + tile, :]
            cv_ref[sl, r0:r0 + tile, :] = acc
        return carry

    lax.fori_loop(0, ns, dw_body, 0)

    cvp = cvp_ref[...]

    def ln_body(i, carry):
        r0 = pl.multiple_of(i * ln_tile, ln_tile)
        cv = jnp.concatenate([cv_ref[sl, pl.ds(r0, ln_tile), :] for sl in range(ns)], axis=1)
        y = _ln(cv + cvp[0:1], cvp[1:2], cvp[2:3])
        o_ref[pl.ds(r0, ln_tile), :] = (y * jax.nn.sigmoid(y)).astype(BF16)
        return carry

    lax.fori_loop(0, length // ln_tile, ln_body, 0)


def _conv_front(xs, mod, w1_all, j, b1, wdw, cvp, *, width):
    nb, length, d = xs.shape
    ns = d // LANES
    tile = 128
    pw_tile = min(512, length)
    ln_tile = min(512, length)
    assert length % pw_tile == 0 and length % ln_tile == 0 and width // 2 < HALO
    per_b = mod.shape[0] == nb and nb > 1
    mod_map = (lambda b: (b, 0, 0)) if per_b else (lambda b: (0, 0, 0))
    return pl.pallas_call(
        functools.partial(_conv_front_kernel, width=width, pw_tile=pw_tile, tile=tile, ln_tile=ln_tile),
        out_shape=jax.ShapeDtypeStruct((nb, length, d), BF16),
        grid=(nb,),
        in_specs=[
            pl.BlockSpec((None, length, d), lambda b: (b, 0, 0)),
            pl.BlockSpec((None, 8, d), mod_map),
            _layer_weight(w1_all, j, 1),
            pl.BlockSpec((1, 2 * d), lambda b: (0, 0)),
            pl.BlockSpec(wdw.shape, lambda b: (0, 0, 0)),
            pl.BlockSpec((8, d), lambda b: (0, 0)),
        ],
        out_specs=pl.BlockSpec((None, length, d), lambda b: (b, 0, 0)),
        scratch_shapes=[pltpu.VMEM((ns, length + 2 * HALO, LANES), F32), pltpu.VMEM((ns, length, LANES), F32)],
        compiler_params=pltpu.CompilerParams(vmem_limit_bytes=VMEM_LIMIT),
        name="conv_front",
    )(xs, mod, w1_all, b1, wdw, cvp)


def _tail_mlp_kernel(*refs, glu, has_pos, alpha, hidden_chunk):
    it = iter(refs)
    x_ref = next(it)
    pos_ref = next(it) if has_pos else None
    m_ref, mod_ref, wm_ref, bm_ref, ln_ref, w1_ref, w2_ref, o_ref = it
    x = x_ref[...]
    if has_pos:
        x = x + pos_ref[...]
    o_ref[...] = _tail_body(x, m_ref[...], mod_ref[...], ln_ref[...], wm_ref, bm_ref, w1_ref, w2_ref,
                            glu=glu, alpha=alpha, hidden_chunk=hidden_chunk)


def _layer_weight(stack, layer, grid_rank):
    zeros = (0,) * (stack.ndim - 1)
    if grid_rank == 1:
        index_map = lambda b: (layer,) + zeros
    else:
        index_map = lambda b, t: (layer,) + zeros
    return pl.BlockSpec((None,) + stack.shape[1:], index_map, pipeline_mode=pl.Buffered(1))


def _tail_mlp(xs, pos, mix_in, mod, wm_all, jm, bm, lnp, w1_all, w2_all, layer, *, glu, alpha):
    nb, length, d = xs.shape
    tm = min(512, length)
    assert length % tm == 0
    has_pos = pos is not None
    per_b = mod.shape[0] == nb and nb > 1
    mod_map = (lambda b, t: (b, 0, 0)) if per_b else (lambda b, t: (0, 0, 0))
    const = lambda b, t: (0, 0)
    in_specs = [pl.BlockSpec((None, tm, d), lambda b, t: (b, t, 0))]
    args = [xs]
    if has_pos:
        in_specs.append(pl.BlockSpec((tm, d), lambda b, t: (t, 0)))
        args.append(pos)
    in_specs += [
        pl.BlockSpec((None, tm, d), lambda b, t: (b, t, 0)),
        pl.BlockSpec((None, 8, d), mod_map),
        _layer_weight(wm_all, jm, 2),
        pl.BlockSpec(bm.shape, const),
        pl.BlockSpec((8, d), const),
        _layer_weight(w1_all, layer, 2),
        _layer_weight(w2_all, layer, 2),
    ]
    args += [mix_in, mod, wm_all, bm, lnp, w1_all, w2_all]
    return pl.pallas_call(
        functools.partial(_tail_mlp_kernel, glu=glu, has_pos=has_pos, alpha=alpha,
                          hidden_chunk=min(1024, w1_all.shape[2])),
        out_shape=jax.ShapeDtypeStruct((nb, length, d), F32),
        grid=(nb, length // tm),
        in_specs=in_specs,
        out_specs=pl.BlockSpec((None, tm, d), lambda b, t: (b, t, 0)),
        compiler_params=pltpu.CompilerParams(vmem_limit_bytes=VMEM_LIMIT),
        name="tail_glu" if glu else "tail_pw",
    )(*args)


def _sincos_1d(pos, dim):
    quarter = dim // 2
    omega = POS_TEMP ** (-jnp.arange(quarter, dtype=F32) / quarter)
    ang = pos[:, None] * omega[None, :]
    return jnp.concatenate([jnp.sin(ang), jnp.cos(ang)], axis=-1)


def _grid_pos_embed(rows, dim):
    row_idx = jnp.repeat(jnp.arange(rows), GRID_W).astype(F32)
    col_idx = jnp.tile(jnp.arange(GRID_W), rows).astype(F32)
    return jnp.concatenate([_sincos_1d(row_idx, dim // 2), _sincos_1d(col_idx, dim // 2)], axis=-1)


def _rows8(*rows):
    d = rows[0].shape[-1]
    out = jnp.stack([r.astype(F32).reshape(d) for r in rows])
    return jnp.concatenate([out, jnp.zeros((8 - len(rows), d), F32)], axis=0)


def kernel(x, c, ctx, c_ctx, w_ada, b_ada, ln_gain, ln_bias, s5_lam_re, s5_lam_im, s5_log_dt, s5_b_re, s5_b_im, s5_c_re, s5_c_im, s5_d, s5_w_glu, s5_b_glu, cv_w_pw1, cv_b_pw1, cv_w_dw, cv_b_dw, cv_ln_g, cv_ln_b, cv_w_pw2, cv_b_pw2, mlp_w1, mlp_w2):
    bsz, length, d = x.shape
    lctx = ctx.shape[1]
    depth = w_ada.shape[0]
    width = cv_w_dw.shape[1]
    alpha = (2.0 * depth) ** 0.25
    mixers = ("s5", "conv")
    kinds = [mixers[i % 2] for i in range(depth)]
    assert length == ROWS and NQ * lctx == ROWS and bsz % NQ == 0 and d % LANES == 0
    nbc = bsz // NQ

    pos = _grid_pos_embed(length // GRID_W, d).astype(x.dtype)
    nrow = -(-(bsz + 1) // 8) * 8
    cc = jnp.concatenate([c.astype(F32), c_ctx.astype(F32)[None], jnp.zeros((nrow - bsz - 1, d), F32)], axis=0)
    mods = _ada_mods(cc, w_ada.astype(F32), b_ada.astype(F32))
    mods = jnp.pad(mods.reshape(depth, nrow, 6, d), ((0, 0), (0, 0), (0, 2), (0, 0)))

    w1_all = mlp_w1.astype(BF16)
    w2_all = mlp_w2.astype(BF16)
    glu_all = s5_w_glu.astype(BF16)
    pw1_all = cv_w_pw1.astype(BF16)
    pw2_all = cv_w_pw2.astype(BF16)

    s5_j = 0
    cv_j = 0
    for i, kind in enumerate(kinds):
        ctx_needed_later = any(k == "s5" for k in kinds[i + 1:])
        mod_lat = mods[i, :bsz]
        mod_ctx = mods[i, bsz:bsz + 1]
        lnp = _rows8(ln_gain[i, 0], ln_gain[i, 1], ln_bias[i, 0], ln_bias[i, 1])
        pos_i = pos if i == 0 else None
        if kind == "s5":
            j = s5_j
            s5_j += 1
            mats = _s5_matrices(s5_lam_re[j], s5_lam_im[j], s5_log_dt[j], s5_b_re[j], s5_b_im[j],
                                s5_c_re[j], s5_c_im[j])
            dskip = s5_d[j].astype(F32).reshape(1, d)
            bm = s5_b_glu[j].astype(F32).reshape(1, 2 * d)
            zc, hfin = _s5_core(ctx.reshape(nbc, ROWS, d), None, mod_ctx, dskip, mats, None, chain=False,
                                want_z=ctx_needed_later)
            h0 = jnp.transpose(hfin, (0, 1, 3, 2, 4)).reshape(d // LANES, bsz, 2 * SLAB_G, LANES)
            z = _s5_core(x, pos_i, mod_lat, dskip, mats, h0, chain=True)
            x_new = _tail_mlp(x, pos_i, z, mod_lat, glu_all, j, bm, lnp, w1_all, w2_all, i, glu=True, alpha=alpha)
            if ctx_needed_later:
                ctx = _tail_mlp(ctx, None, zc.reshape(bsz, lctx, d), mod_ctx, glu_all, j, bm, lnp, w1_all, w2_all, i,
                                glu=True, alpha=alpha)
            x = x_new
        else:
            j = cv_j
            cv_j += 1
            bp1 = cv_b_pw1[j].astype(F32).reshape(1, 2 * d)
            assert pos_i is None
            wdw = jnp.concatenate([cv_w_dw[j].astype(F32), jnp.zeros((-width % 8, d), F32)], axis=0)
            wdw = jnp.transpose(wdw.reshape(-1, d // LANES, LANES), (1, 0, 2))
            cvp = _rows8(cv_b_dw[j], cv_ln_g[j], cv_ln_b[j])
            bm = cv_b_pw2[j].astype(F32).reshape(1, d)
            a = _conv_front(x, mod_lat, pw1_all, j, bp1, wdw, cvp, width=width)
            x_new = _tail_mlp(x, None, a, mod_lat, pw2_all, j, bm, lnp, w1_all, w2_all, i, glu=False, alpha=alpha)
            if ctx_needed_later:
                ac = _conv_front(ctx, mod_ctx, pw1_all, j, bp1, wdw, cvp, width=width)
                ctx = _tail_mlp(ctx, None, ac, mod_ctx, pw2_all, j, bm, lnp, w1_all, w2_all, i, glu=False, alpha=alpha)
            x = x_new
    return x
```

```python
import functools
import math

import jax
import jax.numpy as jnp
from jax import lax
from jax.experimental import pallas as pl
from jax.experimental.pallas import tpu as pltpu

F32 = jnp.float32
BF16 = jnp.bfloat16

LANES = 128
MXU_DIM = 256
S5_CH = 16
CHUNK = 16
CK = CHUNK * S5_CH
SLAB_G = LANES // S5_CH
NQ = 8
NJ = NQ * CHUNK
ROWS = NJ * CHUNK
PITCH = 24
TAB_ROWS = 40
HALO = 16
TAIL_PARTS = 2
GRID_W = 64
POS_TEMP = 10000.0
LN_EPS = 1e-5
LAMBDA_RE_MAX = -1e-4
VMEM_LIMIT = 56 * 1024 * 1024
HI = lax.Precision.HIGHEST


def _ln(v, g, b):
    mu = jnp.mean(v, axis=-1, keepdims=True)
    d = v - mu
    var = jnp.mean(d * d, axis=-1, keepdims=True)
    return d * lax.rsqrt(var + LN_EPS) * g + b


def _ada_kernel(cc_ref, w_ref, b_ref, o_ref):
    cc = cc_ref[...]
    s = cc * jax.nn.sigmoid(cc)
    o_ref[...] = jnp.dot(s, w_ref[...], preferred_element_type=F32, precision=HI) + b_ref[...]


def _ada_mods(cc, w_ada, b_ada):
    depth, d, d6 = w_ada.shape
    rows = cc.shape[0]
    tn = 1536 if d6 % 1536 == 0 else d6
    return pl.pallas_call(
        _ada_kernel,
        out_shape=jax.ShapeDtypeStruct((depth, rows, d6), F32),
        grid=(depth, d6 // tn),
        in_specs=[
            pl.BlockSpec((rows, d), lambda i, j: (0, 0)),
            pl.BlockSpec((None, d, tn), lambda i, j: (i, 0, j)),
            pl.BlockSpec((None, 1, tn), lambda i, j: (i, 0, j)),
        ],
        out_specs=pl.BlockSpec((None, rows, tn), lambda i, j: (i, 0, j)),
        compiler_params=pltpu.CompilerParams(vmem_limit_bytes=VMEM_LIMIT),
        name="ada_mods",
    )(cc, w_ada, b_ada.reshape(depth, 1, d6))


def _s5_core_kernel(*refs, chain, has_pos, want_z):
    it = iter(refs)
    x_ref = next(it)
    pos_ref = next(it) if has_pos else None
    mod_ref = next(it)
    dsk_ref = next(it)
    mi_ref = next(it)
    ws_ref = next(it)
    wo_ref = next(it)
    tab_ref = next(it)
    h0_ref = next(it) if chain else None
    z_ref = next(it) if want_z else None
    hfin_ref = None if chain else next(it)
    u_ref, r_ref, s_ref, e_ref, yt_ref, y_ref, f_ref, fw_ref, e16_ref = it
    half = LANES // 2

    mod = mod_ref[...]
    scale1 = 1.0 + mod[1:2]
    shift1 = mod[0:1]
    for j in range(NJ):
        x = x_ref[j * CHUNK:(j + 1) * CHUNK, :]
        if has_pos:
            x = x + pos_ref[j * CHUNK:(j + 1) * CHUNK, :]
        u_ref[j * PITCH:j * PITCH + CHUNK, :] = x * scale1 + shift1

    for s in range(CHUNK):
        xs = u_ref[pl.ds(s, NJ, stride=PITCH), :]
        r_ref[:, s] = xs.T.reshape(SLAB_G, S5_CH, NJ).astype(BF16)

    for g in range(SLAB_G):
        rg = r_ref[g].reshape(CK, NJ)
        if want_z:
            yt_ref[g] = jnp.dot(mi_ref[g], rg, preferred_element_type=F32)
        st = lax.dot_general(rg, ws_ref[g], (((0,), (0,)), ((), ())),
                             preferred_element_type=F32)
        for q in range(NQ):
            s_ref[g, 0, q * PITCH:q * PITCH + CHUNK, :] = st[q * CHUNK:(q + 1) * CHUNK, :LANES]
            s_ref[g, 1, q * PITCH:q * PITCH + CHUNK, :] = st[q * CHUNK:(q + 1) * CHUNK, LANES:]

    def cstep(a1, a2, e, w, s, sw):
        return a1 * e + a2 * w + s, a1 * w - a2 * e + sw

    def scan_dir(g, d):
        sg = s_ref.at[g, d]
        eg = e_ref.at[g, d]
        gd = g * 2 + d
        tab = lambda r: tab_ref[r, gd:gd + 1, :]
        a1 = tab(0)
        a2 = tab(1)
        e = jnp.zeros((NQ, LANES), F32)
        w = e
        order = range(CHUNK) if d == 0 else range(CHUNK - 1, -1, -1)
        eloc = []
        for jp in order:
            eloc.append((jp, e))
            s = sg[pl.ds(jp, NQ, stride=PITCH), :]
            e, w = cstep(a1, a2, e, w, s, pltpu.roll(s, half, 1))
        if not chain:
            if want_z:
                for jp, el in eloc:
                    eg[pl.ds(jp, NQ, stride=PITCH), :] = el
            hfin_ref[gd] = e
            return
        f_ref[gd] = e
        fw_ref[gd] = w
        b1 = tab(2)
        b2 = tab(3)
        cst = jnp.broadcast_to(h0_ref[gd:gd + 1, :], (NQ, LANES))
        cstw = pltpu.roll(cst, half, 1)
        qs = range(NQ) if d == 0 else range(NQ - 1, -1, -1)
        for q in qs:
            e16_ref[gd, q:q + 1, :] = cst[0:1, :]
            cst, cstw = cstep(b1, b2, cst, cstw, f_ref[gd, q:q + 1, :], fw_ref[gd, q:q + 1, :])
        e16 = e16_ref[gd]
        e16r = pltpu.roll(e16, half, 1)
        for jp, el in eloc:
            eg[pl.ds(jp, NQ, stride=PITCH), :] = el + tab(8 + jp) * e16 + tab(8 + CHUNK + jp) * e16r

    for g in range(SLAB_G):
        scan_dir(g, 0)
        scan_dir(g, 1)

    if not want_z:
        return

    for g in range(SLAB_G):
        ef = jnp.concatenate([e_ref[g, 0, q * PITCH:q * PITCH + CHUNK, :] for q in range(NQ)], axis=0)
        eb = jnp.concatenate([e_ref[g, 1, q * PITCH:q * PITCH + CHUNK, :] for q in range(NQ)], axis=0)
        et = jnp.concatenate([ef, eb], axis=1).astype(BF16)
        yo = lax.dot_general(wo_ref[g], et, (((1,), (1,)), ((), ())),
                             preferred_element_type=F32)
        yt_ref[g] = yt_ref[g] + yo

    for t in range(CHUNK):
        zt = yt_ref[:, t * S5_CH:(t + 1) * S5_CH, :].reshape(LANES, NJ)
        y_ref[pl.ds(t, NJ, stride=PITCH), :] = zt.T

    dsk = dsk_ref[...]
    for j in range(NJ):
        y = y_ref[j * PITCH:j * PITCH + CHUNK, :] + dsk * u_ref[j * PITCH:j * PITCH + CHUNK, :]
        z_ref[j * CHUNK:(j + 1) * CHUNK, :] = (0.5 * y * (1.0 + lax.erf(y * (1.0 / math.sqrt(2.0))))).astype(BF16)


def _s5_core(xs, pos, mod, dskip, mats, h0, *, chain, want_z=True):
    nb, rows, d = xs.shape
    assert rows == ROWS and d % LANES == 0 and (want_z or not chain)
    ns = d // LANES
    mi, ws, wo, tab = mats
    has_pos = pos is not None
    per_b = mod.shape[0] == nb and nb > 1
    in_specs = [pl.BlockSpec((None, ROWS, LANES), lambda k, b: (b, 0, k))]
    args = [xs]
    if has_pos:
        in_specs.append(pl.BlockSpec((ROWS, LANES), lambda k, b: (0, k)))
        args.append(pos)
    if per_b:
        in_specs.append(pl.BlockSpec((None, 8, LANES), lambda k, b: (b, 0, k)))
    else:
        in_specs.append(pl.BlockSpec((None, 8, LANES), lambda k, b: (0, 0, k)))
    args.append(mod)
    in_specs += [
        pl.BlockSpec((1, LANES), lambda k, b: (0, k)),
        pl.BlockSpec((SLAB_G, CK, CK), lambda k, b: (k, 0, 0)),
        pl.BlockSpec((SLAB_G, CK, 256), lambda k, b: (k, 0, 0)),
        pl.BlockSpec((SLAB_G, CK, 256), lambda k, b: (k, 0, 0)),
        pl.BlockSpec((None, TAB_ROWS, 2 * SLAB_G, LANES), lambda k, b: (k, 0, 0, 0)),
    ]
    args += [dskip, mi, ws, wo, tab]
    out_shape = []
    out_specs = []
    if want_z:
        out_shape.append(jax.ShapeDtypeStruct((nb, ROWS, d), BF16))
        out_specs.append(pl.BlockSpec((None, ROWS, LANES), lambda k, b: (b, 0, k)))
    if chain:
        in_specs.append(pl.BlockSpec((None, None, 2 * SLAB_G, LANES), lambda k, b: (k, b, 0, 0)))
        args.append(h0)
    else:
        out_shape.append(jax.ShapeDtypeStruct((ns, nb, 2 * SLAB_G, NQ, LANES), F32))
        out_specs.append(pl.BlockSpec((None, None, 2 * SLAB_G, NQ, LANES), lambda k, b: (k, b, 0, 0, 0)))
    scratch = [
        pltpu.VMEM((NJ * PITCH, LANES), F32),
        pltpu.VMEM((SLAB_G, CHUNK, S5_CH, NJ), BF16),
        pltpu.VMEM((SLAB_G, 2, NQ * PITCH, LANES), F32),
        pltpu.VMEM((SLAB_G, 2, NQ * PITCH, LANES), F32),
        pltpu.VMEM((SLAB_G, CK, NJ), F32),
        pltpu.VMEM((NJ * PITCH, LANES), F32),
        pltpu.VMEM((2 * SLAB_G, NQ, LANES), F32),
        pltpu.VMEM((2 * SLAB_G, NQ, LANES), F32),
        pltpu.VMEM((2 * SLAB_G, NQ, LANES), F32),
    ]
    res = pl.pallas_call(
        functools.partial(_s5_core_kernel, chain=chain, has_pos=has_pos, want_z=want_z),
        out_shape=out_shape,
        grid=(ns, nb),
        in_specs=in_specs,
        out_specs=out_specs,
        scratch_shapes=scratch,
        compiler_params=pltpu.CompilerParams(vmem_limit_bytes=VMEM_LIMIT),
        name="s5_core_lat" if chain else ("s5_core_ctx" if want_z else "s5_state_ctx"),
    )(*args)
    if chain:
        return res[0]
    return (res[0], res[1]) if want_z else (None, res[0])


def _s5_prep_kernel(lr_ref, li_ref, ldt_ref, bri_ref, bir_ref, cri_ref, cir_ref,
                    mi_ref, ws_ref, wo_ref, tab_ref, t1_ref, t2_ref, u1_ref, u2_ref, x_ref, cf_ref):
    nrow = 2 * SLAB_G
    half = LANES // 2
    lane = lax.broadcasted_iota(jnp.int32, (nrow, LANES), 1)
    first = lane < half
    fwd_row = lax.broadcasted_iota(jnp.int32, (nrow, LANES), 0) % 2 == 0

    lr = jnp.minimum(lr_ref[...], LAMBDA_RE_MAX)
    li = li_ref[...]
    dt = jnp.exp(ldt_ref[...])
    mag = jnp.exp(lr * dt)
    ar = mag * jnp.cos(li * dt)
    ai = mag * jnp.sin(li * dt)
    den = lr * lr + li * li
    nr = ar - 1.0
    coef_re = (nr * lr + ai * li) / den
    coef_im = (ai * lr - nr * li) / den
    cf_ref[0] = coef_re
    cf_ref[1] = jnp.where(first, -coef_im, coef_im)

    def powers(a_re, a_im):
        a1 = a_re
        a2 = jnp.where(first, -a_im, a_im)
        p = jnp.where(first, 1.0, 0.0).astype(F32)
        w = jnp.where(first, 0.0, 1.0).astype(F32)
        out = [(p, w)]
        for _ in range(CHUNK):
            p, w = a1 * p + a2 * w, a1 * w - a2 * p
            out.append((p, w))
        return out

    pw = powers(ar, ai)
    for k, (p, w) in enumerate(pw):
        t1_ref[k] = jnp.where(first, p, w)
        t2_ref[k] = jnp.where(first, -w, p)
        u1_ref[k] = jnp.where(first, p, -w)
        u2_ref[k] = jnp.where(first, -w, -p)
    p16, w16 = pw[CHUNK]
    a16_re = jnp.where(first, p16, w16)
    a16_im = jnp.where(first, w16, p16)
    qw = powers(a16_re, a16_im)
    q1 = [jnp.where(first, p, w) for p, w in qw]
    q2 = [jnp.where(first, -w, p) for p, w in qw]
    tab_ref[0] = q1[1]
    tab_ref[1] = q2[1]
    tab_ref[2] = q1[CHUNK]
    tab_ref[3] = q2[CHUNK]
    for r in range(4, 8):
        tab_ref[r] = jnp.zeros((nrow, LANES), F32)
    for jp in range(CHUNK):
        tab_ref[8 + jp] = jnp.where(fwd_row, q1[jp], q1[CHUNK - 1 - jp])
        tab_ref[8 + CHUNK + jp] = jnp.where(fwd_row, q2[jp], q2[CHUNK - 1 - jp])

    lane2 = lax.broadcasted_iota(jnp.int32, (S5_CH, CK), 1)

    def group_body(g, carry):
        mt = [None] * CHUNK
        for d in range(2):
            r = g * 2 + d
            row = lambda v: jnp.broadcast_to(v, (S5_CH, LANES))
            bri = bri_ref[g, d]
            bir = bir_ref[g, d]
            cc1 = row(cf_ref[0, pl.ds(r, 1), :])
            cc2 = row(cf_ref[1, pl.ds(r, 1), :])
            bb_ri = cc1 * bri + cc2 * bir
            bb_ir = cc1 * bir - cc2 * bri
            cri = cri_ref[g, d]
            cir = cir_ref[g, d]
            for s in range(CHUNK):
                k = CHUNK - 1 - s if d == 0 else s
                xk = row(t1_ref[k, pl.ds(r, 1), :]) * bb_ri + row(t2_ref[k, pl.ds(r, 1), :]) * bb_ir
                x_ref[s * S5_CH:(s + 1) * S5_CH, :] = xk
                ws_ref[g, s * S5_CH:(s + 1) * S5_CH, d * LANES:(d + 1) * LANES] = xk.astype(BF16)
            for t in range(CHUNK):
                k = t + 1 if d == 0 else CHUNK - t
                wk = row(u1_ref[k, pl.ds(r, 1), :]) * cri + row(u2_ref[k, pl.ds(r, 1), :]) * cir
                wo_ref[g, t * S5_CH:(t + 1) * S5_CH, d * LANES:(d + 1) * LANES] = wk.astype(BF16)
            cneg = jnp.where(first[:S5_CH], cri, -cri)
            krow = lax.dot_general(cneg, x_ref[...], (((1,), (1,)), ((), ())),
                                   preferred_element_type=F32, precision=HI)
            for t in range(CHUNK):
                if d == 0:
                    sh = (CHUNK - 1 - t) * S5_CH
                    rolled = pltpu.roll(krow, (CK - sh) % CK, 1) if sh else krow
                    mt[t] = jnp.where(lane2 < (t + 1) * S5_CH, rolled, 0.0)
                else:
                    sh = t * S5_CH
                    rolled = pltpu.roll(krow, sh, 1) if sh else krow
                    mt[t] = mt[t] + jnp.where(lane2 >= sh, rolled, 0.0)
        for t in range(CHUNK):
            mi_ref[g, t * S5_CH:(t + 1) * S5_CH, :] = mt[t].astype(BF16)
        return carry

    lax.fori_loop(0, SLAB_G, group_body, 0, unroll=True)


def _s5_matrices(lam_re, lam_im, log_dt, b_re, b_im, c_re, c_im):
    _, g, p = lam_re.shape
    assert 2 * p == LANES and g % SLAB_G == 0
    ns = g // SLAB_G
    nrow = 2 * SLAB_G

    def rows(v):
        v = jnp.transpose(v.astype(F32), (1, 0, 2)).reshape(ns, nrow, p)
        return jnp.concatenate([v, v], axis=-1)

    def pairs(vr, vi):
        return jnp.transpose(jnp.concatenate([vr.astype(F32), vi.astype(F32)], axis=-1), (1, 0, 2, 3))

    ldt = jnp.broadcast_to(log_dt.astype(F32)[..., None], (2, g, p))
    b_re_t = jnp.swapaxes(b_re, -1, -2)
    b_im_t = jnp.swapaxes(b_im, -1, -2)
    blk4 = pl.BlockSpec((SLAB_G, 2, S5_CH, LANES), lambda k: (k, 0, 0, 0))
    blk3 = pl.BlockSpec((None, nrow, LANES), lambda k: (k, 0, 0))
    mat = jax.ShapeDtypeStruct((g, CK, CK), BF16)
    mat_spec = pl.BlockSpec((SLAB_G, CK, CK), lambda k: (k, 0, 0))
    return pl.pallas_call(
        _s5_prep_kernel,
        out_shape=[mat, mat, mat, jax.ShapeDtypeStruct((ns, TAB_ROWS, nrow, LANES), F32)],
        grid=(ns,),
        in_specs=[blk3, blk3, blk3, blk4, blk4, blk4, blk4],
        out_specs=[mat_spec, mat_spec, mat_spec,
                   pl.BlockSpec((None, TAB_ROWS, nrow, LANES), lambda k: (k, 0, 0, 0))],
        scratch_shapes=[pltpu.VMEM((CHUNK + 1, nrow, LANES), F32)] * 4
        + [pltpu.VMEM((CK, LANES), F32), pltpu.VMEM((2, nrow, LANES), F32)],
        compiler_params=pltpu.CompilerParams(vmem_limit_bytes=VMEM_LIMIT),
        name="s5_prep",
    )(rows(lam_re), rows(lam_im), rows(ldt), pairs(b_re_t, b_im_t), pairs(b_im_t, b_re_t),
      pairs(c_re, c_im), pairs(c_im, c_re))


def _tail_body(x_ref, pos_ref, m_ref, mod, lnp, wm_ref, bm_ref, w1_ref, w2_ref, o_ref, x1_ref, h_ref,
               *, glu, alpha, hidden_chunk):
    tm, d = x_ref.shape
    dff = w1_ref.shape[1]
    parts = TAIL_PARTS if tm % (TAIL_PARTS * MXU_DIM) == 0 else 1
    rows = tm // parts
    for part in range(parts):
        rs = slice(part * rows, (part + 1) * rows)
        x = x_ref[rs, :]
        if pos_ref is not None:
            x = x + pos_ref[rs, :]
        zz = jnp.dot(m_ref[rs, :], wm_ref[...], preferred_element_type=F32) + bm_ref[...]
        if glu:
            mix = zz[:, :d] * jax.nn.sigmoid(zz[:, d:])
        else:
            mix = zz
        x1 = _ln(alpha * x + mod[2:3] * mix, lnp[0:1], lnp[2:3])
        x1_ref[rs, :] = x1
        h_ref[rs, :] = (x1 * (1.0 + mod[4:5]) + mod[3:4]).astype(BF16)
    for part in range(parts):
        rs = slice(part * rows, (part + 1) * rows)
        h = h_ref[rs, :]
        acc = jnp.zeros((rows, d), F32)
        for c in range(dff // hidden_chunk):
            lo = c * hidden_chunk
            hid = jnp.dot(h, w1_ref[:, lo:lo + hidden_chunk], preferred_element_type=F32)
            hid = jnp.square(jnp.maximum(hid, 0.0)).astype(BF16)
            acc = acc + jnp.dot(hid, w2_ref[lo:lo + hidden_chunk, :], preferred_element_type=F32)
        o_ref[rs, :] = _ln(alpha * x1_ref[rs, :] + mod[5:6] * acc, lnp[1:2], lnp[3:4])


def _conv_front_kernel(x_ref, mod_ref, w1_ref, b1_ref, wdw_ref, cvp_ref, o_ref, a_ref, cv_ref,
                       *, width, pw_tile, tile, ln_tile):
    length, d = x_ref.shape
    ns = d // LANES
    half = width // 2
    pad = HALO
    a_ref[:, 0:pad, :] = jnp.zeros((ns, pad, LANES), F32)
    a_ref[:, pad + length:, :] = jnp.zeros((ns, pad, LANES), F32)
    mod = mod_ref[...]
    shift = mod[0:1]
    scale1 = 1.0 + mod[1:2]

    def pw_body(i, carry):
        r0 = pl.multiple_of(i * pw_tile, pw_tile)
        h = (x_ref[pl.ds(r0, pw_tile), :] * scale1 + shift).astype(BF16)
        a = jnp.dot(h, w1_ref[...], preferred_element_type=F32) + b1_ref[...]
        a = a[:, :d] * jax.nn.sigmoid(a[:, d:])
        for sl in range(ns):
            a_ref[sl, pl.ds(pad + r0, pw_tile), :] = a[:, sl * LANES:(sl + 1) * LANES]
        return carry

    lax.fori_loop(0, length // pw_tile, pw_body, 0)

    def dw_body(sl, carry):
        for r0 in range(0, length, tile):
            acc = jnp.zeros((tile, LANES), F32)
            for k in range(width):
                lo = r0 + pad - half + k
                acc = acc + wdw_ref[sl, k:k + 1, :] * a_ref[sl, lo:lo + tile, :]
            cv_ref[sl, r0:r0 + tile, :] = acc
        return carry

    lax.fori_loop(0, ns, dw_body, 0)

    cvp = cvp_ref[...]

    def ln_body(i, carry):
        r0 = pl.multiple_of(i * ln_tile, ln_tile)
        cv = jnp.concatenate([cv_ref[sl, pl.ds(r0, ln_tile), :] for sl in range(ns)], axis=1)
        y = _ln(cv + cvp[0:1], cvp[1:2], cvp[2:3])
        o_ref[pl.ds(r0, ln_tile), :] = (y * jax.nn.sigmoid(y)).astype(BF16)
        return carry

    lax.fori_loop(0, length // ln_tile, ln_body, 0)


def _conv_front(xs, mod, w1_all, j, b1, wdw, cvp, *, width):
    nb, length, d = xs.shape
    ns = d // LANES
    tile = 128
    pw_tile = min(512, length)
    ln_tile = min(512, length)
    assert length % pw_tile == 0 and length % ln_tile == 0 and width // 2 < HALO
    per_b = mod.shape[0] == nb and nb > 1
    mod_map = (lambda b: (b, 0, 0)) if per_b else (lambda b: (0, 0, 0))
    return pl.pallas_call(
        functools.partial(_conv_front_kernel, width=width, pw_tile=pw_tile, tile=tile, ln_tile=ln_tile),
        out_shape=jax.ShapeDtypeStruct((nb, length, d), BF16),
        grid=(nb,),
        in_specs=[
            pl.BlockSpec((None, length, d), lambda b: (b, 0, 0)),
            pl.BlockSpec((None, 8, d), mod_map),
            _layer_weight(w1_all, j, 1),
            pl.BlockSpec((1, 2 * d), lambda b: (0, 0)),
            pl.BlockSpec(wdw.shape, lambda b: (0, 0, 0)),
            pl.BlockSpec((8, d), lambda b: (0, 0)),
        ],
        out_specs=pl.BlockSpec((None, length, d), lambda b: (b, 0, 0)),
        scratch_shapes=[pltpu.VMEM((ns, length + 2 * HALO, LANES), F32), pltpu.VMEM((ns, length, LANES), F32)],
        compiler_params=pltpu.CompilerParams(vmem_limit_bytes=VMEM_LIMIT),
        name="conv_front",
    )(xs, mod, w1_all, b1, wdw, cvp)


def _tail_mlp_kernel(*refs, glu, has_pos, alpha, hidden_chunk):
    it = iter(refs)
    x_ref = next(it)
    pos_ref = next(it) if has_pos else None
    m_ref, mod_ref, wm_ref, bm_ref, ln_ref, w1_ref, w2_ref, o_ref, x1_ref, h_ref = it
    _tail_body(x_ref, pos_ref, m_ref, mod_ref[...], ln_ref[...], wm_ref, bm_ref, w1_ref, w2_ref, o_ref,
               x1_ref, h_ref, glu=glu, alpha=alpha, hidden_chunk=hidden_chunk)


def _layer_weight(stack, layer, grid_rank):
    zeros = (0,) * (stack.ndim - 1)
    if grid_rank == 1:
        index_map = lambda b: (layer,) + zeros
    else:
        index_map = lambda b, t: (layer,) + zeros
    return pl.BlockSpec((None,) + stack.shape[1:], index_map, pipeline_mode=pl.Buffered(1))


def _tail_mlp(xs, pos, mix_in, mod, wm_all, jm, bm, lnp, w1_all, w2_all, layer, *, glu, alpha):
    nb, length, d = xs.shape
    tm = min(512, length)
    assert length % tm == 0
    has_pos = pos is not None
    per_b = mod.shape[0] == nb and nb > 1
    mod_map = (lambda b, t: (b, 0, 0)) if per_b else (lambda b, t: (0, 0, 0))
    const = lambda b, t: (0, 0)
    in_specs = [pl.BlockSpec((None, tm, d), lambda b, t: (b, t, 0))]
    args = [xs]
    if has_pos:
        in_specs.append(pl.BlockSpec((tm, d), lambda b, t: (t, 0)))
        args.append(pos)
    in_specs += [
        pl.BlockSpec((None, tm, d), lambda b, t: (b, t, 0)),
        pl.BlockSpec((None, 8, d), mod_map),
        _layer_weight(wm_all, jm, 2),
        pl.BlockSpec(bm.shape, const),
        pl.BlockSpec((8, d), const),
        _layer_weight(w1_all, layer, 2),
        _layer_weight(w2_all, layer, 2),
    ]
    args += [mix_in, mod, wm_all, bm, lnp, w1_all, w2_all]
    return pl.pallas_call(
        functools.partial(_tail_mlp_kernel, glu=glu, has_pos=has_pos, alpha=alpha,
                          hidden_chunk=min(1024, w1_all.shape[2])),
        out_shape=jax.ShapeDtypeStruct((nb, length, d), F32),
        grid=(nb, length // tm),
        in_specs=in_specs,
        out_specs=pl.BlockSpec((None, tm, d), lambda b, t: (b, t, 0)),
        scratch_shapes=[pltpu.VMEM((tm, d), F32), pltpu.VMEM((tm, d), BF16)],
        compiler_params=pltpu.CompilerParams(vmem_limit_bytes=VMEM_LIMIT),
        name="tail_glu" if glu else "tail_pw",
    )(*args)


def _sincos_1d(pos, dim):
    quarter = dim // 2
    omega = POS_TEMP ** (-jnp.arange(quarter, dtype=F32) / quarter)
    ang = pos[:, None] * omega[None, :]
    return jnp.concatenate([jnp.sin(ang), jnp.cos(ang)], axis=-1)


def _grid_pos_embed(rows, dim):
    row_idx = jnp.repeat(jnp.arange(rows), GRID_W).astype(F32)
    col_idx = jnp.tile(jnp.arange(GRID_W), rows).astype(F32)
    return jnp.concatenate([_sincos_1d(row_idx, dim // 2), _sincos_1d(col_idx, dim // 2)], axis=-1)


def _rows8(*rows):
    d = rows[0].shape[-1]
    out = jnp.stack([r.astype(F32).reshape(d) for r in rows])
    return jnp.concatenate([out, jnp.zeros((8 - len(rows), d), F32)], axis=0)


def kernel(x, c, ctx, c_ctx, w_ada, b_ada, ln_gain, ln_bias, s5_lam_re, s5_lam_im, s5_log_dt, s5_b_re, s5_b_im, s5_c_re, s5_c_im, s5_d, s5_w_glu, s5_b_glu, cv_w_pw1, cv_b_pw1, cv_w_dw, cv_b_dw, cv_ln_g, cv_ln_b, cv_w_pw2, cv_b_pw2, mlp_w1, mlp_w2):
    bsz, length, d = x.shape
    lctx = ctx.shape[1]
    depth = w_ada.shape[0]
    width = cv_w_dw.shape[1]
    alpha = (2.0 * depth) ** 0.25
    mixers = ("s5", "conv")
    kinds = [mixers[i % 2] for i in range(depth)]
    assert length == ROWS and NQ * lctx == ROWS and bsz % NQ == 0 and d % LANES == 0
    nbc = bsz // NQ

    pos = _grid_pos_embed(length // GRID_W, d).astype(x.dtype)
    nrow = -(-(bsz + 1) // 8) * 8
    cc = jnp.concatenate([c.astype(F32), c_ctx.astype(F32)[None], jnp.zeros((nrow - bsz - 1, d), F32)], axis=0)
    mods = _ada_mods(cc, w_ada.astype(F32), b_ada.astype(F32))
    mods = jnp.pad(mods.reshape(depth, nrow, 6, d), ((0, 0), (0, 0), (0, 2), (0, 0)))

    w1_all = mlp_w1.astype(BF16)
    w2_all = mlp_w2.astype(BF16)
    glu_all = s5_w_glu.astype(BF16)
    pw1_all = cv_w_pw1.astype(BF16)
    pw2_all = cv_w_pw2.astype(BF16)

    s5_j = 0
    cv_j = 0
    for i, kind in enumerate(kinds):
        ctx_needed_later = any(k == "s5" for k in kinds[i + 1:])
        mod_lat = mods[i, :bsz]
        mod_ctx = mods[i, bsz:bsz + 1]
        lnp = _rows8(ln_gain[i, 0], ln_gain[i, 1], ln_bias[i, 0], ln_bias[i, 1])
        pos_i = pos if i == 0 else None
        if kind == "s5":
            j = s5_j
            s5_j += 1
            mats = _s5_matrices(s5_lam_re[j], s5_lam_im[j], s5_log_dt[j], s5_b_re[j], s5_b_im[j],
                                s5_c_re[j], s5_c_im[j])
            dskip = s5_d[j].astype(F32).reshape(1, d)
            bm = s5_b_glu[j].astype(F32).reshape(1, 2 * d)
            zc, hfin = _s5_core(ctx.reshape(nbc, ROWS, d), None, mod_ctx, dskip, mats, None, chain=False,
                                want_z=ctx_needed_later)
            h0 = jnp.transpose(hfin, (0, 1, 3, 2, 4)).reshape(d // LANES, bsz, 2 * SLAB_G, LANES)
            z = _s5_core(x, pos_i, mod_lat, dskip, mats, h0, chain=True)
            x_new = _tail_mlp(x, pos_i, z, mod_lat, glu_all, j, bm, lnp, w1_all, w2_all, i, glu=True, alpha=alpha)
            if ctx_needed_later:
                ctx = _tail_mlp(ctx, None, zc.reshape(bsz, lctx, d), mod_ctx, glu_all, j, bm, lnp, w1_all, w2_all, i,
                                glu=True, alpha=alpha)
            x = x_new
        else:
            j = cv_j
            cv_j += 1
            bp1 = cv_b_pw1[j].astype(F32).reshape(1, 2 * d)
            assert pos_i is None
            wdw = jnp.concatenate([cv_w_dw[j].astype(F32), jnp.zeros((-width % 8, d), F32)], axis=0)
            wdw = jnp.transpose(wdw.reshape(-1, d // LANES, LANES), (1, 0, 2))
            cvp = _rows8(cv_b_dw[j], cv_ln_g[j], cv_ln_b[j])
            bm = cv_b_pw2[j].astype(F32).reshape(1, d)
            a = _conv_front(x, mod_lat, pw1_all, j, bp1, wdw, cvp, width=width)
            x_new = _tail_mlp(x, None, a, mod_lat, pw2_all, j, bm, lnp, w1_all, w2_all, i, glu=False, alpha=alpha)
            if ctx_needed_later:
                ac = _conv_front(ctx, mod_ctx, pw1_all, j, bp1, wdw, cvp, width=width)
                ctx = _tail_mlp(ctx, None, ac, mod_ctx, pw2_all, j, bm, lnp, w1_all, w2_all, i, glu=False, alpha=alpha)
            x = x_new
    return x
```

```python
import functools
import math

import jax
import jax.numpy as jnp
from jax import lax
from jax.experimental import pallas as pl
from jax.experimental.pallas import tpu as pltpu

F32 = jnp.float32
BF16 = jnp.bfloat16

LANES = 128
MXU_DIM = 256
S5_CH = 16
CHUNK = 16
CK = CHUNK * S5_CH
SLAB_G = LANES // S5_CH
NQ = 8
NJ = NQ * CHUNK
ROWS = NJ * CHUNK
PITCH = 24
TAB_ROWS = 40
HALO = 16
TAIL_PARTS = 2
GRID_W = 64
POS_TEMP = 10000.0
LN_EPS = 1e-5
LAMBDA_RE_MAX = -1e-4
VMEM_LIMIT = 56 * 1024 * 1024
HI = lax.Precision.HIGHEST


def _ln(v, g, b):
    mu = jnp.mean(v, axis=-1, keepdims=True)
    d = v - mu
    var = jnp.mean(d * d, axis=-1, keepdims=True)
    return d * lax.rsqrt(var + LN_EPS) * g + b


def _ada_kernel(cc_ref, w_ref, b_ref, o_ref):
    cc = cc_ref[...]
    s = cc * jax.nn.sigmoid(cc)
    o_ref[...] = jnp.dot(s, w_ref[...], preferred_element_type=F32, precision=HI) + b_ref[...]


def _ada_mods(cc, w_ada, b_ada):
    depth, d, d6 = w_ada.shape
    rows = cc.shape[0]
    tn = 1536 if d6 % 1536 == 0 else d6
    return pl.pallas_call(
        _ada_kernel,
        out_shape=jax.ShapeDtypeStruct((depth, rows, d6), F32),
        grid=(depth, d6 // tn),
        in_specs=[
            pl.BlockSpec((rows, d), lambda i, j: (0, 0)),
            pl.BlockSpec((None, d, tn), lambda i, j: (i, 0, j)),
            pl.BlockSpec((None, 1, tn), lambda i, j: (i, 0, j)),
        ],
        out_specs=pl.BlockSpec((None, rows, tn), lambda i, j: (i, 0, j)),
        compiler_params=pltpu.CompilerParams(vmem_limit_bytes=VMEM_LIMIT),
        name="ada_mods",
    )(cc, w_ada, b_ada.reshape(depth, 1, d6))


def _s5_core_kernel(*refs, chain, has_pos, want_z):
    it = iter(refs)
    x_ref = next(it)
    pos_ref = next(it) if has_pos else None
    mod_ref = next(it)
    dsk_ref = next(it)
    mi_ref = next(it)
    ws_ref = next(it)
    wo_ref = next(it)
    tab_ref = next(it)
    h0_ref = next(it) if chain else None
    z_ref = next(it) if want_z else None
    hfin_ref = None if chain else next(it)
    u_ref, r_ref, s_ref, e_ref, yt_ref, y_ref, f_ref, fw_ref, e16_ref = it
    half = LANES // 2

    mod = mod_ref[...]
    scale1 = 1.0 + mod[1:2]
    shift1 = mod[0:1]
    for j in range(NJ):
        x = x_ref[j * CHUNK:(j + 1) * CHUNK, :]
        if has_pos:
            x = x + pos_ref[j * CHUNK:(j + 1) * CHUNK, :]
        u_ref[j * PITCH:j * PITCH + CHUNK, :] = x * scale1 + shift1

    for s in range(CHUNK):
        xs = u_ref[pl.ds(s, NJ, stride=PITCH), :]
        r_ref[:, s] = xs.T.reshape(SLAB_G, S5_CH, NJ).astype(BF16)

    for g in range(SLAB_G):
        rg = r_ref[g].reshape(CK, NJ)
        if want_z:
            yt_ref[g] = jnp.dot(mi_ref[g], rg, preferred_element_type=F32)
        st = lax.dot_general(rg, ws_ref[g], (((0,), (0,)), ((), ())),
                             preferred_element_type=F32)
        for q in range(NQ):
            s_ref[g, 0, q * PITCH:q * PITCH + CHUNK, :] = st[q * CHUNK:(q + 1) * CHUNK, :LANES]
            s_ref[g, 1, q * PITCH:q * PITCH + CHUNK, :] = st[q * CHUNK:(q + 1) * CHUNK, LANES:]

    def cstep(a1, a2, e, w, s, sw):
        return a1 * e + a2 * w + s, a1 * w - a2 * e + sw

    def scan_dir(g, d):
        sg = s_ref.at[g, d]
        eg = e_ref.at[g, d]
        gd = g * 2 + d
        tab = lambda r: tab_ref[r, gd:gd + 1, :]
        a1 = tab(0)
        a2 = tab(1)
        e = jnp.zeros((NQ, LANES), F32)
        w = e
        order = range(CHUNK) if d == 0 else range(CHUNK - 1, -1, -1)
        eloc = []
        for jp in order:
            eloc.append((jp, e))
            s = sg[pl.ds(jp, NQ, stride=PITCH), :]
            e, w = cstep(a1, a2, e, w, s, pltpu.roll(s, half, 1))
        if not chain:
            if want_z:
                for jp, el in eloc:
                    eg[pl.ds(jp, NQ, stride=PITCH), :] = el
            hfin_ref[gd] = e
            return
        f_ref[gd] = e
        fw_ref[gd] = w
        b1 = tab(2)
        b2 = tab(3)
        cst = jnp.broadcast_to(h0_ref[gd:gd + 1, :], (NQ, LANES))
        cstw = pltpu.roll(cst, half, 1)
        qs = range(NQ) if d == 0 else range(NQ - 1, -1, -1)
        for q in qs:
            e16_ref[gd, q:q + 1, :] = cst[0:1, :]
            cst, cstw = cstep(b1, b2, cst, cstw, f_ref[gd, q:q + 1, :], fw_ref[gd, q:q + 1, :])
        e16 = e16_ref[gd]
        e16r = pltpu.roll(e16, half, 1)
        for jp, el in eloc:
            eg[pl.ds(jp, NQ, stride=PITCH), :] = el + tab(8 + jp) * e16 + tab(8 + CHUNK + jp) * e16r

    for g in range(SLAB_G):
        scan_dir(g, 0)
        scan_dir(g, 1)

    if not want_z:
        return

    for g in range(SLAB_G):
        ef = jnp.concatenate([e_ref[g, 0, q * PITCH:q * PITCH + CHUNK, :] for q in range(NQ)], axis=0)
        eb = jnp.concatenate([e_ref[g, 1, q * PITCH:q * PITCH + CHUNK, :] for q in range(NQ)], axis=0)
        et = jnp.concatenate([ef, eb], axis=1).astype(BF16)
        yo = lax.dot_general(wo_ref[g], et, (((1,), (1,)), ((), ())),
                             preferred_element_type=F32)
        yt_ref[g] = yt_ref[g] + yo

    for t in range(CHUNK):
        zt = yt_ref[:, t * S5_CH:(t + 1) * S5_CH, :].reshape(LANES, NJ)
        y_ref[pl.ds(t, NJ, stride=PITCH), :] = zt.T

    dsk = dsk_ref[...]
    for j in range(NJ):
        y = y_ref[j * PITCH:j * PITCH + CHUNK, :] + dsk * u_ref[j * PITCH:j * PITCH + CHUNK, :]
        z_ref[j * CHUNK:(j + 1) * CHUNK, :] = (0.5 * y * (1.0 + lax.erf(y * (1.0 / math.sqrt(2.0))))).astype(BF16)


def _s5_core(xs, pos, mod, dskip, mats, h0, *, chain, want_z=True):
    nb, rows, d = xs.shape
    assert rows == ROWS and d % LANES == 0 and (want_z or not chain)
    ns = d // LANES
    mi, ws, wo, tab = mats
    has_pos = pos is not None
    per_b = mod.shape[0] == nb and nb > 1
    in_specs = [pl.BlockSpec((None, ROWS, LANES), lambda k, b: (b, 0, k))]
    args = [xs]
    if has_pos:
        in_specs.append(pl.BlockSpec((ROWS, LANES), lambda k, b: (0, k)))
        args.append(pos)
    if per_b:
        in_specs.append(pl.BlockSpec((None, 8, LANES), lambda k, b: (b, 0, k)))
    else:
        in_specs.append(pl.BlockSpec((None, 8, LANES), lambda k, b: (0, 0, k)))
    args.append(mod)
    in_specs += [
        pl.BlockSpec((1, LANES), lambda k, b: (0, k)),
        pl.BlockSpec((SLAB_G, CK, CK), lambda k, b: (k, 0, 0)),
        pl.BlockSpec((SLAB_G, CK, 256), lambda k, b: (k, 0, 0)),
        pl.BlockSpec((SLAB_G, CK, 256), lambda k, b: (k, 0, 0)),
        pl.BlockSpec((None, TAB_ROWS, 2 * SLAB_G, LANES), lambda k, b: (k, 0, 0, 0)),
    ]
    args += [dskip, mi, ws, wo, tab]
    out_shape = []
    out_specs = []
    if want_z:
        out_shape.append(jax.ShapeDtypeStruct((nb, ROWS, d), BF16))
        out_specs.append(pl.BlockSpec((None, ROWS, LANES), lambda k, b: (b, 0, k)))
    if chain:
        in_specs.append(pl.BlockSpec((None, None, 2 * SLAB_G, LANES), lambda k, b: (k, b, 0, 0)))
        args.append(h0)
    else:
        out_shape.append(jax.ShapeDtypeStruct((ns, nb, 2 * SLAB_G, NQ, LANES), F32))
        out_specs.append(pl.BlockSpec((None, None, 2 * SLAB_G, NQ, LANES), lambda k, b: (k, b, 0, 0, 0)))
    scratch = [
        pltpu.VMEM((NJ * PITCH, LANES), F32),
        pltpu.VMEM((SLAB_G, CHUNK, S5_CH, NJ), BF16),
        pltpu.VMEM((SLAB_G, 2, NQ * PITCH, LANES), F32),
        pltpu.VMEM((SLAB_G, 2, NQ * PITCH, LANES), F32),
        pltpu.VMEM((SLAB_G, CK, NJ), F32),
        pltpu.VMEM((NJ * PITCH, LANES), F32),
        pltpu.VMEM((2 * SLAB_G, NQ, LANES), F32),
        pltpu.VMEM((2 * SLAB_G, NQ, LANES), F32),
        pltpu.VMEM((2 * SLAB_G, NQ, LANES), F32),
    ]
    res = pl.pallas_call(
        functools.partial(_s5_core_kernel, chain=chain, has_pos=has_pos, want_z=want_z),
        out_shape=out_shape,
        grid=(ns, nb),
        in_specs=in_specs,
        out_specs=out_specs,
        scratch_shapes=scratch,
        compiler_params=pltpu.CompilerParams(vmem_limit_bytes=VMEM_LIMIT),
        name="s5_core_lat" if chain else ("s5_core_ctx" if want_z else "s5_state_ctx"),
    )(*args)
    if chain:
        return res[0]
    return (res[0], res[1]) if want_z else (None, res[0])


def _s5_prep_kernel(lr_ref, li_ref, ldt_ref, bri_ref, bir_ref, cri_ref, cir_ref,
                    mi_ref, ws_ref, wo_ref, tab_ref, t1_ref, t2_ref, u1_ref, u2_ref, x_ref, cf_ref):
    nrow = 2 * SLAB_G
    half = LANES // 2
    lane = lax.broadcasted_iota(jnp.int32, (nrow, LANES), 1)
    first = lane < half
    fwd_row = lax.broadcasted_iota(jnp.int32, (nrow, LANES), 0) % 2 == 0

    lr = jnp.minimum(lr_ref[...], LAMBDA_RE_MAX)
    li = li_ref[...]
    dt = jnp.exp(ldt_ref[...])
    mag = jnp.exp(lr * dt)
    ar = mag * jnp.cos(li * dt)
    ai = mag * jnp.sin(li * dt)
    den = lr * lr + li * li
    nr = ar - 1.0
    coef_re = (nr * lr + ai * li) / den
    coef_im = (ai * lr - nr * li) / den
    cf_ref[0] = coef_re
    cf_ref[1] = jnp.where(first, -coef_im, coef_im)

    def powers(a_re, a_im):
        a1 = a_re
        a2 = jnp.where(first, -a_im, a_im)
        p = jnp.where(first, 1.0, 0.0).astype(F32)
        w = jnp.where(first, 0.0, 1.0).astype(F32)
        out = [(p, w)]
        for _ in range(CHUNK):
            p, w = a1 * p + a2 * w, a1 * w - a2 * p
            out.append((p, w))
        return out

    pw = powers(ar, ai)
    for k, (p, w) in enumerate(pw):
        t1_ref[k] = jnp.where(first, p, w)
        t2_ref[k] = jnp.where(first, -w, p)
        u1_ref[k] = jnp.where(first, p, -w)
        u2_ref[k] = jnp.where(first, -w, -p)
    p16, w16 = pw[CHUNK]
    a16_re = jnp.where(first, p16, w16)
    a16_im = jnp.where(first, w16, p16)
    qw = powers(a16_re, a16_im)
    q1 = [jnp.where(first, p, w) for p, w in qw]
    q2 = [jnp.where(first, -w, p) for p, w in qw]
    tab_ref[0] = q1[1]
    tab_ref[1] = q2[1]
    tab_ref[2] = q1[CHUNK]
    tab_ref[3] = q2[CHUNK]
    for r in range(4, 8):
        tab_ref[r] = jnp.zeros((nrow, LANES), F32)
    for jp in range(CHUNK):
        tab_ref[8 + jp] = jnp.where(fwd_row, q1[jp], q1[CHUNK - 1 - jp])
        tab_ref[8 + CHUNK + jp] = jnp.where(fwd_row, q2[jp], q2[CHUNK - 1 - jp])

    lane2 = lax.broadcasted_iota(jnp.int32, (S5_CH, CK), 1)

    def group_body(g, carry):
        mt = [None] * CHUNK
        for d in range(2):
            r = g * 2 + d
            row = lambda v: jnp.broadcast_to(v, (S5_CH, LANES))
            bri = bri_ref[g, d]
            bir = bir_ref[g, d]
            cc1 = row(cf_ref[0, pl.ds(r, 1), :])
            cc2 = row(cf_ref[1, pl.ds(r, 1), :])
            bb_ri = cc1 * bri + cc2 * bir
            bb_ir = cc1 * bir - cc2 * bri
            cri = cri_ref[g, d]
            cir = cir_ref[g, d]
            for s in range(CHUNK):
                k = CHUNK - 1 - s if d == 0 else s
                xk = row(t1_ref[k, pl.ds(r, 1), :]) * bb_ri + row(t2_ref[k, pl.ds(r, 1), :]) * bb_ir
                x_ref[s * S5_CH:(s + 1) * S5_CH, :] = xk
                ws_ref[g, s * S5_CH:(s + 1) * S5_CH, d * LANES:(d + 1) * LANES] = xk.astype(BF16)
            for t in range(CHUNK):
                k = t + 1 if d == 0 else CHUNK - t
                wk = row(u1_ref[k, pl.ds(r, 1), :]) * cri + row(u2_ref[k, pl.ds(r, 1), :]) * cir
                wo_ref[g, t * S5_CH:(t + 1) * S5_CH, d * LANES:(d + 1) * LANES] = wk.astype(BF16)
            cneg = jnp.where(first[:S5_CH], cri, -cri)
            krow = lax.dot_general(cneg, x_ref[...], (((1,), (1,)), ((), ())),
                                   preferred_element_type=F32, precision=HI)
            for t in range(CHUNK):
                if d == 0:
                    sh = (CHUNK - 1 - t) * S5_CH
                    rolled = pltpu.roll(krow, (CK - sh) % CK, 1) if sh else krow
                    mt[t] = jnp.where(lane2 < (t + 1) * S5_CH, rolled, 0.0)
                else:
                    sh = t * S5_CH
                    rolled = pltpu.roll(krow, sh, 1) if sh else krow
                    mt[t] = mt[t] + jnp.where(lane2 >= sh, rolled, 0.0)
        for t in range(CHUNK):
            mi_ref[g, t * S5_CH:(t + 1) * S5_CH, :] = mt[t].astype(BF16)
        return carry

    lax.fori_loop(0, SLAB_G, group_body, 0, unroll=True)


def _s5_matrices(lam_re, lam_im, log_dt, b_re, b_im, c_re, c_im):
    _, g, p = lam_re.shape
    assert 2 * p == LANES and g % SLAB_G == 0
    ns = g // SLAB_G
    nrow = 2 * SLAB_G

    def rows(v):
        v = jnp.transpose(v.astype(F32), (1, 0, 2)).reshape(ns, nrow, p)
        return jnp.concatenate([v, v], axis=-1)

    def pairs(vr, vi):
        return jnp.transpose(jnp.concatenate([vr.astype(F32), vi.astype(F32)], axis=-1), (1, 0, 2, 3))

    ldt = jnp.broadcast_to(log_dt.astype(F32)[..., None], (2, g, p))
    b_re_t = jnp.swapaxes(b_re, -1, -2)
    b_im_t = jnp.swapaxes(b_im, -1, -2)
    blk4 = pl.BlockSpec((SLAB_G, 2, S5_CH, LANES), lambda k: (k, 0, 0, 0))
    blk3 = pl.BlockSpec((None, nrow, LANES), lambda k: (k, 0, 0))
    mat = jax.ShapeDtypeStruct((g, CK, CK), BF16)
    mat_spec = pl.BlockSpec((SLAB_G, CK, CK), lambda k: (k, 0, 0))
    return pl.pallas_call(
        _s5_prep_kernel,
        out_shape=[mat, mat, mat, jax.ShapeDtypeStruct((ns, TAB_ROWS, nrow, LANES), F32)],
        grid=(ns,),
        in_specs=[blk3, blk3, blk3, blk4, blk4, blk4, blk4],
        out_specs=[mat_spec, mat_spec, mat_spec,
                   pl.BlockSpec((None, TAB_ROWS, nrow, LANES), lambda k: (k, 0, 0, 0))],
        scratch_shapes=[pltpu.VMEM((CHUNK + 1, nrow, LANES), F32)] * 4
        + [pltpu.VMEM((CK, LANES), F32), pltpu.VMEM((2, nrow, LANES), F32)],
        compiler_params=pltpu.CompilerParams(vmem_limit_bytes=VMEM_LIMIT),
        name="s5_prep",
    )(rows(lam_re), rows(lam_im), rows(ldt), pairs(b_re_t, b_im_t), pairs(b_im_t, b_re_t),
      pairs(c_re, c_im), pairs(c_im, c_re))


def _tail_body(x_ref, pos_ref, m_ref, cvp_ref, mod, lnp, wm_ref, bm_ref, w1_ref, w2_ref, o_ref, x1_ref, h_ref,
               *, glu, alpha, hidden_chunk):
    tm, d = x_ref.shape
    dff = w1_ref.shape[1]
    parts = TAIL_PARTS if tm % (TAIL_PARTS * MXU_DIM) == 0 else 1
    rows = tm // parts
    for part in range(parts):
        rs = slice(part * rows, (part + 1) * rows)
        x = x_ref[rs, :]
        if pos_ref is not None:
            x = x + pos_ref[rs, :]
        m = m_ref[rs, :]
        if cvp_ref is not None:
            cvp = cvp_ref[...]
            y = _ln(m + cvp[0:1], cvp[1:2], cvp[2:3])
            m = (y * jax.nn.sigmoid(y)).astype(BF16)
        zz = jnp.dot(m, wm_ref[...], preferred_element_type=F32) + bm_ref[...]
        if glu:
            mix = zz[:, :d] * jax.nn.sigmoid(zz[:, d:])
        else:
            mix = zz
        x1 = _ln(alpha * x + mod[2:3] * mix, lnp[0:1], lnp[2:3])
        x1_ref[rs, :] = x1
        h_ref[rs, :] = (x1 * (1.0 + mod[4:5]) + mod[3:4]).astype(BF16)
    for part in range(parts):
        rs = slice(part * rows, (part + 1) * rows)
        h = h_ref[rs, :]
        acc = jnp.zeros((rows, d), F32)
        for c in range(dff // hidden_chunk):
            lo = c * hidden_chunk
            hid = jnp.dot(h, w1_ref[:, lo:lo + hidden_chunk], preferred_element_type=F32)
            hid = jnp.square(jnp.maximum(hid, 0.0)).astype(BF16)
            acc = acc + jnp.dot(hid, w2_ref[lo:lo + hidden_chunk, :], preferred_element_type=F32)
        o_ref[rs, :] = _ln(alpha * x1_ref[rs, :] + mod[5:6] * acc, lnp[1:2], lnp[3:4])


def _conv_front_kernel(x_ref, mod_ref, w1_ref, b1_ref, wdw_ref, o_ref, a_ref, *, width, pw_tile, tile):
    length, d = x_ref.shape
    ns = d // LANES
    half = width // 2
    pad = HALO
    a_ref[:, 0:pad, :] = jnp.zeros((ns, pad, LANES), F32)
    a_ref[:, pad + length:, :] = jnp.zeros((ns, pad, LANES), F32)
    mod = mod_ref[...]
    shift = mod[0:1]
    scale1 = 1.0 + mod[1:2]

    def pw_body(i, carry):
        r0 = pl.multiple_of(i * pw_tile, pw_tile)
        h = (x_ref[pl.ds(r0, pw_tile), :] * scale1 + shift).astype(BF16)
        a = jnp.dot(h, w1_ref[...], preferred_element_type=F32) + b1_ref[...]
        a = a[:, :d] * jax.nn.sigmoid(a[:, d:])
        for sl in range(ns):
            a_ref[sl, pl.ds(pad + r0, pw_tile), :] = a[:, sl * LANES:(sl + 1) * LANES]
        return carry

    lax.fori_loop(0, length // pw_tile, pw_body, 0)

    def dw_body(sl, carry):
        lane0 = pl.multiple_of(sl * LANES, LANES)
        for r0 in range(0, length, tile):
            acc = jnp.zeros((tile, LANES), F32)
            for k in range(width):
                lo = r0 + pad - half + k
                acc = acc + wdw_ref[sl, k:k + 1, :] * a_ref[sl, lo:lo + tile, :]
            o_ref[r0:r0 + tile, pl.ds(lane0, LANES)] = acc
        return carry

    lax.fori_loop(0, ns, dw_body, 0)


def _conv_front(xs, mod, w1_all, j, b1, wdw, *, width):
    nb, length, d = xs.shape
    ns = d // LANES
    tile = 128
    pw_tile = min(512, length)
    assert length % pw_tile == 0 and width // 2 < HALO
    per_b = mod.shape[0] == nb and nb > 1
    mod_map = (lambda b: (b, 0, 0)) if per_b else (lambda b: (0, 0, 0))
    return pl.pallas_call(
        functools.partial(_conv_front_kernel, width=width, pw_tile=pw_tile, tile=tile),
        out_shape=jax.ShapeDtypeStruct((nb, length, d), F32),
        grid=(nb,),
        in_specs=[
            pl.BlockSpec((None, length, d), lambda b: (b, 0, 0)),
            pl.BlockSpec((None, 8, d), mod_map),
            _layer_weight(w1_all, j, 1),
            pl.BlockSpec((1, 2 * d), lambda b: (0, 0)),
            pl.BlockSpec(wdw.shape, lambda b: (0, 0, 0)),
        ],
        out_specs=pl.BlockSpec((None, length, d), lambda b: (b, 0, 0)),
        scratch_shapes=[pltpu.VMEM((ns, length + 2 * HALO, LANES), F32)],
        compiler_params=pltpu.CompilerParams(vmem_limit_bytes=VMEM_LIMIT),
        name="conv_front",
    )(xs, mod, w1_all, b1, wdw)


def _tail_mlp_kernel(*refs, glu, has_pos, has_cvp, alpha, hidden_chunk):
    it = iter(refs)
    x_ref = next(it)
    pos_ref = next(it) if has_pos else None
    m_ref = next(it)
    cvp_ref = next(it) if has_cvp else None
    mod_ref, wm_ref, bm_ref, ln_ref, w1_ref, w2_ref, o_ref, x1_ref, h_ref = it
    _tail_body(x_ref, pos_ref, m_ref, cvp_ref, mod_ref[...], ln_ref[...], wm_ref, bm_ref, w1_ref, w2_ref, o_ref,
               x1_ref, h_ref, glu=glu, alpha=alpha, hidden_chunk=hidden_chunk)


def _layer_weight(stack, layer, grid_rank):
    zeros = (0,) * (stack.ndim - 1)
    if grid_rank == 1:
        index_map = lambda b: (layer,) + zeros
    else:
        index_map = lambda b, t: (layer,) + zeros
    return pl.BlockSpec((None,) + stack.shape[1:], index_map, pipeline_mode=pl.Buffered(1))


def _tail_mlp(xs, pos, mix_in, mod, wm_all, jm, bm, lnp, w1_all, w2_all, layer, *, glu, alpha, cvp=None):
    nb, length, d = xs.shape
    tm = min(512, length)
    assert length % tm == 0
    has_pos = pos is not None
    per_b = mod.shape[0] == nb and nb > 1
    mod_map = (lambda b, t: (b, 0, 0)) if per_b else (lambda b, t: (0, 0, 0))
    const = lambda b, t: (0, 0)
    in_specs = [pl.BlockSpec((None, tm, d), lambda b, t: (b, t, 0))]
    args = [xs]
    if has_pos:
        in_specs.append(pl.BlockSpec((tm, d), lambda b, t: (t, 0)))
        args.append(pos)
    in_specs.append(pl.BlockSpec((None, tm, d), lambda b, t: (b, t, 0)))
    args.append(mix_in)
    if cvp is not None:
        in_specs.append(pl.BlockSpec((8, d), const))
        args.append(cvp)
    in_specs += [
        pl.BlockSpec((None, 8, d), mod_map),
        _layer_weight(wm_all, jm, 2),
        pl.BlockSpec(bm.shape, const),
        pl.BlockSpec((8, d), const),
        _layer_weight(w1_all, layer, 2),
        _layer_weight(w2_all, layer, 2),
    ]
    args += [mod, wm_all, bm, lnp, w1_all, w2_all]
    return pl.pallas_call(
        functools.partial(_tail_mlp_kernel, glu=glu, has_pos=has_pos, has_cvp=cvp is not None, alpha=alpha,
                          hidden_chunk=min(1024, w1_all.shape[2])),
        out_shape=jax.ShapeDtypeStruct((nb, length, d), F32),
        grid=(nb, length // tm),
        in_specs=in_specs,
        out_specs=pl.BlockSpec((None, tm, d), lambda b, t: (b, t, 0)),
        scratch_shapes=[pltpu.VMEM((tm, d), F32), pltpu.VMEM((tm, d), BF16)],
        compiler_params=pltpu.CompilerParams(vmem_limit_bytes=VMEM_LIMIT),
        name="tail_glu" if glu else "tail_pw",
    )(*args)


def _sincos_1d(pos, dim):
    quarter = dim // 2
    omega = POS_TEMP ** (-jnp.arange(quarter, dtype=F32) / quarter)
    ang = pos[:, None] * omega[None, :]
    return jnp.concatenate([jnp.sin(ang), jnp.cos(ang)], axis=-1)


def _grid_pos_embed(rows, dim):
    row_idx = jnp.repeat(jnp.arange(rows), GRID_W).astype(F32)
    col_idx = jnp.tile(jnp.arange(GRID_W), rows).astype(F32)
    return jnp.concatenate([_sincos_1d(row_idx, dim // 2), _sincos_1d(col_idx, dim // 2)], axis=-1)


def _rows8(*rows):
    d = rows[0].shape[-1]
    out = jnp.stack([r.astype(F32).reshape(d) for r in rows])
    return jnp.concatenate([out, jnp.zeros((8 - len(rows), d), F32)], axis=0)


def kernel(x, c, ctx, c_ctx, w_ada, b_ada, ln_gain, ln_bias, s5_lam_re, s5_lam_im, s5_log_dt, s5_b_re, s5_b_im, s5_c_re, s5_c_im, s5_d, s5_w_glu, s5_b_glu, cv_w_pw1, cv_b_pw1, cv_w_dw, cv_b_dw, cv_ln_g, cv_ln_b, cv_w_pw2, cv_b_pw2, mlp_w1, mlp_w2):
    bsz, length, d = x.shape
    lctx = ctx.shape[1]
    depth = w_ada.shape[0]
    width = cv_w_dw.shape[1]
    alpha = (2.0 * depth) ** 0.25
    mixers = ("s5", "conv")
    kinds = [mixers[i % 2] for i in range(depth)]
    assert length == ROWS and NQ * lctx == ROWS and bsz % NQ == 0 and d % LANES == 0
    nbc = bsz // NQ

    pos = _grid_pos_embed(length // GRID_W, d).astype(x.dtype)
    nrow = -(-(bsz + 1) // 8) * 8
    cc = jnp.concatenate([c.astype(F32), c_ctx.astype(F32)[None], jnp.zeros((nrow - bsz - 1, d), F32)], axis=0)
    mods = _ada_mods(cc, w_ada.astype(F32), b_ada.astype(F32))
    mods = jnp.pad(mods.reshape(depth, nrow, 6, d), ((0, 0), (0, 0), (0, 2), (0, 0)))

    w1_all = mlp_w1.astype(BF16)
    w2_all = mlp_w2.astype(BF16)
    glu_all = s5_w_glu.astype(BF16)
    pw1_all = cv_w_pw1.astype(BF16)
    pw2_all = cv_w_pw2.astype(BF16)

    s5_j = 0
    cv_j = 0
    for i, kind in enumerate(kinds):
        ctx_needed_later = any(k == "s5" for k in kinds[i + 1:])
        mod_lat = mods[i, :bsz]
        mod_ctx = mods[i, bsz:bsz + 1]
        lnp = _rows8(ln_gain[i, 0], ln_gain[i, 1], ln_bias[i, 0], ln_bias[i, 1])
        pos_i = pos if i == 0 else None
        if kind == "s5":
            j = s5_j
            s5_j += 1
            mats = _s5_matrices(s5_lam_re[j], s5_lam_im[j], s5_log_dt[j], s5_b_re[j], s5_b_im[j],
                                s5_c_re[j], s5_c_im[j])
            dskip = s5_d[j].astype(F32).reshape(1, d)
            bm = s5_b_glu[j].astype(F32).reshape(1, 2 * d)
            zc, hfin = _s5_core(ctx.reshape(nbc, ROWS, d), None, mod_ctx, dskip, mats, None, chain=False,
                                want_z=ctx_needed_later)
            h0 = jnp.transpose(hfin, (0, 1, 3, 2, 4)).reshape(d // LANES, bsz, 2 * SLAB_G, LANES)
            z = _s5_core(x, pos_i, mod_lat, dskip, mats, h0, chain=True)
            x_new = _tail_mlp(x, pos_i, z, mod_lat, glu_all, j, bm, lnp, w1_all, w2_all, i, glu=True, alpha=alpha)
            if ctx_needed_later:
                ctx = _tail_mlp(ctx, None, zc.reshape(bsz, lctx, d), mod_ctx, glu_all, j, bm, lnp, w1_all, w2_all, i,
                                glu=True, alpha=alpha)
            x = x_new
        else:
            j = cv_j
            cv_j += 1
            bp1 = cv_b_pw1[j].astype(F32).reshape(1, 2 * d)
            assert pos_i is None
            wdw = jnp.concatenate([cv_w_dw[j].astype(F32), jnp.zeros((-width % 8, d), F32)], axis=0)
            wdw = jnp.transpose(wdw.reshape(-1, d // LANES, LANES), (1, 0, 2))
            cvp = _rows8(cv_b_dw[j], cv_ln_g[j], cv_ln_b[j])
            bm = cv_b_pw2[j].astype(F32).reshape(1, d)
            a = _conv_front(x, mod_lat, pw1_all, j, bp1, wdw, width=width)
            x_new = _tail_mlp(x, None, a, mod_lat, pw2_all, j, bm, lnp, w1_all, w2_all, i, glu=False, alpha=alpha, cvp=cvp)
            if ctx_needed_later:
                ac = _conv_front(ctx, mod_ctx, pw1_all, j, bp1, wdw, width=width)
                ctx = _tail_mlp(ctx, None, ac, mod_ctx, pw2_all, j, bm, lnp, w1_all, w2_all, i, glu=False, alpha=alpha, cvp=cvp)
            x = x_new
    return x
```
